```python
import jax, jax.numpy as jnp
from jax import lax
import numpy as np

D_MODEL = 4096
BATCH = 1
SEQ = 16384
DEPTH = 4

RWKV_WIDTH = D_MODEL // 2
HEAD_SIZE = 64
N_RWKV_HEADS = RWKV_WIDTH // HEAD_SIZE
FOURIER_WIDTH = D_MODEL - RWKV_WIDTH
N_FOURIER_GROUPS = 8
FOURIER_GROUP = FOURIER_WIDTH // N_FOURIER_GROUPS
DECAY_LORA = max(32, int(round(1.8 * RWKV_WIDTH ** 0.5 / 32)) * 32)
AAA_LORA = max(32, int(round(1.8 * RWKV_WIDTH ** 0.5 / 32)) * 32)
GATE_LORA = max(32, int(round(0.6 * RWKV_WIDTH ** 0.8 / 32)) * 32)
N_SHIFT_COLS = 3 * RWKV_WIDTH + GATE_LORA + 2 * DECAY_LORA + 2 * AAA_LORA
N_IN_COLS = N_SHIFT_COLS + FOURIER_WIDTH
D_FF = int(round(8 * D_MODEL / 3 / 256)) * 256
CONV_WIDTH = 3
LN_EPS = 1e-5
GN_EPS = 64e-5
DN_ALPHA = (2 * DEPTH) ** 0.25
DN_BETA = (8 * DEPTH) ** -0.25

kernel_name = "rwkv7_fnet_hybrid_deepnorm_encoder"


def layer_norm(x, g, b):
    xf = x.astype(jnp.float32)
    mu = jnp.mean(xf, axis=-1, keepdims=True)
    var = jnp.mean(jnp.square(xf - mu), axis=-1, keepdims=True)
    y = (xf - mu) * lax.rsqrt(var + LN_EPS) * g.astype(jnp.float32) + b.astype(jnp.float32)
    return y.astype(x.dtype)


def shift_prev(t):
    return jnp.pad(t[:, :-1], ((0, 0), (1, 0), (0, 0)))


def shift_next(t):
    return jnp.pad(t[:, 1:], ((0, 0), (0, 1), (0, 0)))


def wkv7_scan(r, w, k, v, kk, a, reverse):
    bsz, _, h, n = r.shape
    xs = tuple(jnp.moveaxis(t, 1, 0) for t in (r, w, k, v, kk, a))

    def step(state, inp):
        r_t, w_t, k_t, v_t, kk_t, a_t = inp
        sa = jnp.einsum('bhvk,bhk->bhv', state, -kk_t)
        state = (state * w_t[:, :, None, :]
                 + sa[..., None] * (kk_t * a_t)[:, :, None, :]
                 + v_t[..., None] * k_t[:, :, None, :])
        y_t = jnp.einsum('bhvk,bhk->bhv', state, r_t)
        return state, y_t

    state0 = jnp.zeros((bsz, h, n, n), jnp.float32)
    _, y = lax.scan(step, state0, xs, reverse=reverse)
    return jnp.moveaxis(y, 0, 1)


def rwkv7_mixer(z, decay_base, w_decay_up, aaa_base, w_aaa_up, g_lora_up, k_k, k_a, r_k, lnx_g, lnx_b):
    f32 = jnp.float32
    z = z.astype(f32)
    bsz, seq, _ = z.shape
    W = RWKV_WIDTH
    r = z[..., :W]
    k = z[..., W:2 * W]
    v = z[..., 2 * W:3 * W]
    o = 3 * W
    g_d = z[..., o:o + GATE_LORA]
    o += GATE_LORA
    w_d = z[..., o:o + 2 * DECAY_LORA].reshape(bsz, seq, 2, DECAY_LORA)
    o += 2 * DECAY_LORA
    a_d = z[..., o:o + 2 * AAA_LORA].reshape(bsz, seq, 2, AAA_LORA)

    w_log = -jax.nn.softplus(-(decay_base.astype(f32)
                               + jnp.einsum('bsdl,dlc->bsdc', jnp.tanh(w_d), w_decay_up.astype(f32)))) - 0.5
    decay = jnp.exp(-jnp.exp(w_log))
    a = jax.nn.sigmoid(aaa_base.astype(f32) + jnp.einsum('bsdl,dlc->bsdc', a_d, w_aaa_up.astype(f32)))
    g = jax.nn.sigmoid(g_d) @ g_lora_up.astype(f32)

    def heads(t):
        return t.reshape(bsz, seq, N_RWKV_HEADS, HEAD_SIZE)

    kk = heads(k * k_k.astype(f32))
    kk = kk / jnp.maximum(jnp.linalg.norm(kk, axis=-1, keepdims=True), 1e-12)
    k_dir = k[:, :, None, :] * (1.0 + (a - 1.0) * k_a.astype(f32))
    rh, vh = heads(r), heads(v)
    y = (wkv7_scan(rh, heads(decay[:, :, 0]), heads(k_dir[:, :, 0]), vh, kk, heads(a[:, :, 0]), False)
         + wkv7_scan(rh, heads(decay[:, :, 1]), heads(k_dir[:, :, 1]), vh, kk, heads(a[:, :, 1]), True))
    mu = jnp.mean(y, axis=-1, keepdims=True)
    var = jnp.mean(jnp.square(y - mu), axis=-1, keepdims=True)
    y = ((y - mu) * lax.rsqrt(var + GN_EPS)).reshape(bsz, seq, W) * lnx_g.astype(f32) + lnx_b.astype(f32)
    bonus = jnp.sum(rh * heads(k_dir[:, :, 0] + k_dir[:, :, 1]) * r_k.astype(f32), axis=-1, keepdims=True) * vh
    return (y + bonus.reshape(bsz, seq, W)) * g


def fourier_mixer(f):
    bsz, seq, _ = f.shape
    fg = f.astype(jnp.float32).reshape(bsz, seq, N_FOURIER_GROUPS, FOURIER_GROUP)
    return jnp.fft.fftn(fg, axes=(1, 3), norm="ortho").real.reshape(bsz, seq, FOURIER_WIDTH)


def setup_inputs(seed: int = 0) -> dict:
    key = jax.random.key(seed)
    ks = jax.random.split(key, 32)
    f32 = jnp.float32
    nrm = lambda k, shape, s: (jax.random.normal(k, shape, f32) * s).astype(f32)
    L, W = DEPTH, RWKV_WIDTH
    return {
        "x": nrm(ks[0], (BATCH, SEQ, D_MODEL), 1.0),
        "ln0_g": 1.0 + nrm(ks[1], (D_MODEL,), 0.02),
        "ln0_b": nrm(ks[2], (D_MODEL,), 0.02),
        "w_in": nrm(ks[3], (L, D_MODEL, N_IN_COLS), D_MODEL ** -0.5),
        "mu_shift": jax.random.uniform(ks[4], (L, N_SHIFT_COLS), f32),
        "decay_base": jax.random.uniform(ks[5], (L, 2, W), f32, -6.0, -1.0),
        "w_decay_up": nrm(ks[6], (L, 2, DECAY_LORA, W), 0.1 * DECAY_LORA ** -0.5),
        "aaa_base": nrm(ks[7], (L, 2, W), 0.1),
        "w_aaa_up": nrm(ks[8], (L, 2, AAA_LORA, W), 0.3 * AAA_LORA ** -0.5),
        "g_lora_up": nrm(ks[9], (L, GATE_LORA, W), GATE_LORA ** -0.5),
        "k_k": 0.85 + nrm(ks[10], (L, W), 0.05),
        "k_a": 1.0 + nrm(ks[11], (L, W), 0.05),
        "r_k": nrm(ks[12], (L, N_RWKV_HEADS, HEAD_SIZE), 0.1),
        "lnx_g": 1.0 + nrm(ks[13], (L, W), 0.02),
        "lnx_b": nrm(ks[14], (L, W), 0.02),
        "w_out": nrm(ks[15], (L, D_MODEL, D_MODEL), D_MODEL ** -0.5 * DN_BETA),
        "ln1_g": 1.0 + nrm(ks[16], (L, D_MODEL), 0.02),
        "ln1_b": nrm(ks[17], (L, D_MODEL), 0.02),
        "w_ffn_up": nrm(ks[18], (L, D_MODEL, 2 * D_FF), D_MODEL ** -0.5),
        "ffn_conv_w": nrm(ks[19], (L, CONV_WIDTH, 2 * D_FF), CONV_WIDTH ** -0.5),
        "ffn_conv_b": nrm(ks[20], (L, 2 * D_FF), 0.02),
        "w_ffn_down": nrm(ks[21], (L, D_FF, D_MODEL), D_FF ** -0.5 * DN_BETA),
        "ln2_g": 1.0 + nrm(ks[22], (L, D_MODEL), 0.02),
        "ln2_b": nrm(ks[23], (L, D_MODEL), 0.02),
    }


def reference(x, ln0_g, ln0_b, w_in, mu_shift, decay_base, w_decay_up, aaa_base, w_aaa_up,
              g_lora_up, k_k, k_a, r_k, lnx_g, lnx_b, w_out, ln1_g, ln1_b,
              w_ffn_up, ffn_conv_w, ffn_conv_b, w_ffn_down, ln2_g, ln2_b):
    dt = x.dtype
    h = layer_norm(x, ln0_g, ln0_b)
    for l in range(DEPTH):
        p = h @ w_in[l]
        z = p[..., :N_SHIFT_COLS]
        z = z + (0.5 * (shift_prev(z) + shift_next(z)) - z) * mu_shift[l]
        y_rwkv = rwkv7_mixer(z, decay_base[l], w_decay_up[l], aaa_base[l], w_aaa_up[l], g_lora_up[l],
                             k_k[l], k_a[l], r_k[l].reshape(-1, ) .reshape(N_RWKV_HEADS, HEAD_SIZE),
                             lnx_g[l], lnx_b[l])
        y_four = fourier_mixer(p[..., N_SHIFT_COLS:])
        mix = jnp.concatenate([y_rwkv.astype(dt), y_four.astype(dt)], axis=-1)
        h = layer_norm(DN_ALPHA * h + mix @ w_out[l], ln1_g[l], ln1_b[l])
        u = h @ w_ffn_up[l]
        c = ffn_conv_w[l]
        u = c[0] * shift_prev(u) + c[1] * u + c[2] * shift_next(u) + ffn_conv_b[l]
        gate, val = jnp.split(u, 2, axis=-1)
        ffn = (jax.nn.gelu(gate, approximate=False) * val) @ w_ffn_down[l]
        h = layer_norm(DN_ALPHA * h + ffn, ln2_g[l], ln2_b[l])
    return h
```

```python
import functools
import math

import numpy as np
import jax
import jax.numpy as jnp
from jax import lax
from jax.experimental import pallas as pl
from jax.experimental.pallas import tpu as pltpu

F32 = jnp.float32
BF16 = jnp.bfloat16
HIGHEST = lax.Precision.HIGHEST

LANE = 128
HEAD_SIZE = 64
N_FOURIER_GROUPS = 8
LN_EPS = 1e-5
GN_EPS = 64e-5
CHUNK = 64
VMEM_LIMIT = 56 * 1024 * 1024


def _round_up(n, m):
    return (n + m - 1) // m * m


def _pick(dim, target, align=LANE):
    if dim <= target:
        return dim
    best = None
    for t in range(align, target + 1, align):
        if dim % t == 0:
            best = t
    assert best is not None, (dim, target, align)
    return best


def _params(sem):
    return pltpu.CompilerParams(dimension_semantics=sem, vmem_limit_bytes=VMEM_LIMIT)


def _mm_kernel(x_ref, w_ref, o_ref):
    o_ref[...] = jnp.dot(x_ref[...], w_ref[...], preferred_element_type=F32).astype(o_ref.dtype)


def _mm_acc_kernel(x_ref, w_ref, o_ref, acc_ref):
    k = pl.program_id(2)

    @pl.when(k == 0)
    def _():
        acc_ref[...] = jnp.zeros_like(acc_ref)

    acc_ref[...] += jnp.dot(x_ref[...], w_ref[...], preferred_element_type=F32)

    @pl.when(k == pl.num_programs(2) - 1)
    def _():
        o_ref[...] = acc_ref[...].astype(o_ref.dtype)


def _matmul(x, w, out_dtype, tm=1024, tn=1024, tk=4096, name="mm"):
    M, K = x.shape
    N = w.shape[1]
    tm = _pick(M, tm, 8)
    tn = _pick(N, tn)
    tk = _pick(K, tk)
    if tk == K:
        return pl.pallas_call(
            _mm_kernel,
            grid=(N // tn, M // tm),
            in_specs=[pl.BlockSpec((tm, K), lambda j, i: (i, 0)),
                      pl.BlockSpec((K, tn), lambda j, i: (0, j))],
            out_specs=pl.BlockSpec((tm, tn), lambda j, i: (i, j)),
            out_shape=jax.ShapeDtypeStruct((M, N), out_dtype),
            compiler_params=_params(("parallel", "parallel")),
            name=name,
        )(x, w)
    return pl.pallas_call(
        _mm_acc_kernel,
        grid=(N // tn, M // tm, K // tk),
        in_specs=[pl.BlockSpec((tm, tk), lambda j, i, k: (i, k)),
                  pl.BlockSpec((tk, tn), lambda j, i, k: (k, j))],
        out_specs=pl.BlockSpec((tm, tn), lambda j, i, k: (i, j)),
        out_shape=jax.ShapeDtypeStruct((M, N), out_dtype),
        scratch_shapes=[pltpu.VMEM((tm, tn), F32)],
        compiler_params=_params(("parallel", "parallel", "arbitrary")),
        name=name,
    )(x, w)


def _ln_rows(x, g, b):
    mu = jnp.mean(x, axis=-1, keepdims=True)
    xc = x - mu
    var = jnp.mean(xc * xc, axis=-1, keepdims=True)
    return xc * lax.rsqrt(var + LN_EPS) * g + b


def _ln_kernel(x_ref, g_ref, b_ref, o_ref, ob_ref):
    y = _ln_rows(x_ref[...], g_ref[...], b_ref[...])
    o_ref[...] = y
    ob_ref[...] = y.astype(BF16)


def _ln_res_kernel(h_ref, y_ref, g_ref, b_ref, o_ref, ob_ref, *, alpha):
    y = _ln_rows(alpha * h_ref[...] + y_ref[...], g_ref[...], b_ref[...])
    o_ref[...] = y
    ob_ref[...] = y.astype(BF16)


def _layer_norm(x, g, b, res=None, alpha=None):
    M, D = x.shape
    tm = _pick(M, 256, 8)
    row = pl.BlockSpec((tm, D), lambda i: (i, 0))
    vec = pl.BlockSpec((1, D), lambda i: (0, 0))
    out_shape = (jax.ShapeDtypeStruct((M, D), F32), jax.ShapeDtypeStruct((M, D), BF16))
    if res is None:
        return pl.pallas_call(
            _ln_kernel, grid=(M // tm,), in_specs=[row, vec, vec], out_specs=(row, row),
            out_shape=out_shape, compiler_params=_params(("parallel",)), name="ln",
        )(x, g.reshape(1, D), b.reshape(1, D))
    return pl.pallas_call(
        functools.partial(_ln_res_kernel, alpha=alpha), grid=(M // tm,),
        in_specs=[row, row, vec, vec], out_specs=(row, row),
        out_shape=out_shape, compiler_params=_params(("parallel",)), name="ln_res",
    )(x, res, g.reshape(1, D), b.reshape(1, D))


def _halos(p, T):
    S, N = p.shape
    nb = S // T
    zero = jnp.zeros((1, N), p.dtype)
    prev = jnp.concatenate([zero, p[T - 1:S - 1:T]], axis=0)
    nxt = jnp.concatenate([p[T:S:T], zero], axis=0)
    return prev.reshape(nb, 1, N), nxt.reshape(nb, 1, N)


def _neighbours(z, zp, zn):
    T = z.shape[0]
    row = lax.broadcasted_iota(jnp.int32, (T, 1), 0)
    prev = jnp.where(row == 0, zp, pltpu.roll(z, 1, 0))
    nxt = jnp.where(row == T - 1, zn, pltpu.roll(z, T - 1, 0))
    return prev, nxt


def _token_shift(z, zp, zn, mu):
    prev, nxt = _neighbours(z, zp, zn)
    return z + (0.5 * (prev + nxt) - z) * mu


def _seg_sum(x, bd):
    parts = [jnp.dot(x[:, i * LANE:(i + 1) * LANE], bd, precision=HIGHEST, preferred_element_type=F32)
             for i in range(x.shape[1] // LANE)]
    return parts[0] if len(parts) == 1 else jnp.concatenate(parts, axis=1)


def _sigmoid(x):
    return 1.0 / (1.0 + jnp.exp(-x))


def _softplus(x):
    return jnp.maximum(x, 0.0) + jnp.log1p(jnp.exp(-jnp.abs(x)))


def _prep_kernel(pr, pk, pv, hpr, hpk, hpv, hnr, hnk, hnv, plo, hpl, hnl,
                 mur, muk, muv, mul, dbase, wdu, abase, wau, gup, kkw, kaw, rkw, bd,
                 r_o, v_o, kk_o, wl_o, b_o, kd_o, g_o, bg_o, *, gp, lp):
    r = _token_shift(pr[...], hpr[0], hnr[0], mur[...])
    k = _token_shift(pk[...], hpk[0], hnk[0], muk[...])
    v = _token_shift(pv[...], hpv[0], hnv[0], muv[...])
    zl = _token_shift(plo[...], hpl[0], hnl[0], mul[...])
    bdm = bd[...]

    g = jnp.dot(_sigmoid(zl[:, :gp]).astype(BF16), gup[...], preferred_element_type=F32)
    kk = k * kkw[...]
    kk = kk / jnp.maximum(jnp.sqrt(_seg_sum(kk * kk, bdm)), 1e-12)
    ka = kaw[...]
    ksum = jnp.zeros_like(k)
    for d in range(2):
        wd = jnp.tanh(zl[:, gp + d * lp:gp + (d + 1) * lp]).astype(BF16)
        x = dbase[d:d + 1, :] + jnp.dot(wd, wdu[d], preferred_element_type=F32)
        wl_o[d] = -jnp.exp(-_softplus(-x) - 0.5)
        ad = zl[:, gp + (2 + d) * lp:gp + (3 + d) * lp].astype(BF16)
        a = _sigmoid(abase[d:d + 1, :] + jnp.dot(ad, wau[d], preferred_element_type=F32))
        kd = k * (1.0 + (a - 1.0) * ka)
        kd_o[d] = kd
        b_o[d] = a * kk
        ksum = ksum + kd
    bonus = _seg_sum(r * ksum * rkw[...], bdm) * v
    r_o[...] = r
    v_o[...] = v
    kk_o[...] = kk
    g_o[...] = g
    bg_o[...] = bonus * g


def _rwkv_prep(prkv, plora, lw, T, cw):
    S = prkv.shape[0]
    W = prkv.shape[1] // 3
    LW = plora.shape[1]
    gp, lp = lw["gp"], lw["lp"]
    nb, ncb = S // T, W // cw
    hp_rkv, hn_rkv = _halos(prkv, T)
    hp_l, hn_l = _halos(plora, T)

    def main(off):
        return pl.BlockSpec((T, cw), lambda i, j: (i, j + off * ncb))

    def halo(off):
        return pl.BlockSpec((1, 1, cw), lambda i, j: (i, 0, j + off * ncb))

    def vec(off, rows=1):
        return pl.BlockSpec((rows, cw), lambda i, j: (0, j + off * ncb))

    in_specs = [main(0), main(1), main(2), halo(0), halo(1), halo(2), halo(0), halo(1), halo(2),
                pl.BlockSpec((T, LW), lambda i, j: (i, 0)),
                pl.BlockSpec((1, 1, LW), lambda i, j: (i, 0, 0)),
                pl.BlockSpec((1, 1, LW), lambda i, j: (i, 0, 0)),
                vec(0), vec(1), vec(2),
                pl.BlockSpec((1, LW), lambda i, j: (0, 0)),
                vec(0, 2),
                pl.BlockSpec((2, lp, cw), lambda i, j: (0, 0, j)),
                vec(0, 2),
                pl.BlockSpec((2, lp, cw), lambda i, j: (0, 0, j)),
                pl.BlockSpec((gp, cw), lambda i, j: (0, j)),
                vec(0), vec(0), vec(0),
                pl.BlockSpec((LANE, LANE), lambda i, j: (0, 0))]
    one = pl.BlockSpec((T, cw), lambda i, j: (i, j))
    two = pl.BlockSpec((2, T, cw), lambda i, j: (0, i, j))
    sd1 = jax.ShapeDtypeStruct((S, W), F32)
    sd2 = jax.ShapeDtypeStruct((2, S, W), F32)
    return pl.pallas_call(
        functools.partial(_prep_kernel, gp=gp, lp=lp),
        grid=(nb, ncb),
        in_specs=in_specs,
        out_specs=(one, one, one, two, two, two, one, one),
        out_shape=(sd1, sd1, sd1, sd2, sd2, sd2, sd1, sd1),
        compiler_params=_params(("parallel", "parallel")),
        name="rwkv_prep",
    )(prkv, prkv, prkv, hp_rkv, hp_rkv, hp_rkv, hn_rkv, hn_rkv, hn_rkv, plora, hp_l, hn_l,
      lw["mu_rkv"], lw["mu_rkv"], lw["mu_rkv"], lw["mu_lora"], lw["decay_base"], lw["w_decay_up"],
      lw["aaa_base"], lw["w_aaa_up"], lw["g_up"], lw["k_k"], lw["k_a"], lw["r_k"], lw["bd"])


def _dot_nt(a, b):
    return lax.dot_general(a, b, (((1,), (1,)), ((), ())), preferred_element_type=F32)


def _dot_tn(a, b):
    return lax.dot_general(a, b, (((0,), (0,)), ((), ())), preferred_element_type=F32)


def _dot(a, b):
    return jnp.dot(a, b, preferred_element_type=F32)


def _scan_kernel(r_ref, v_ref, kk_ref, wl_ref, b_ref, kd_ref, y_ref,
                 state, at_s, rt_s, bt_s, kt_s, em_s, et_s, *, C, n_slabs, hs):
    d = pl.program_id(0)
    c = pl.program_id(1)

    @pl.when(c == 0)
    def _():
        state[...] = jnp.zeros_like(state)

    sgn = 1 - 2 * d
    ri = lax.broadcasted_iota(jnp.int32, (C, C), 0)
    ci = lax.broadcasted_iota(jnp.int32, (C, C), 1)
    tri = ((ri - ci) * sgn >= 0).astype(F32)
    wl = wl_ref[0]
    cum = jnp.dot(tri, wl, precision=HIGHEST, preferred_element_type=F32)
    tot = jnp.sum(wl, axis=0, keepdims=True)
    m = 0.5 * tot
    e_iv = jnp.exp(m - cum)
    at_s[...] = (-kk_ref[...] * jnp.exp(cum - wl - m)).astype(BF16)
    rt_s[...] = (r_ref[...] * jnp.exp(cum - m)).astype(BF16)
    bt_s[...] = (b_ref[0] * e_iv).astype(BF16)
    kt_s[...] = (kd_ref[0] * e_iv).astype(BF16)
    em_s[...] = jnp.exp(m)
    et_s[...] = jnp.exp(tot)

    n_heads = LANE // hs
    R = n_heads * C
    lane = lax.broadcasted_iota(jnp.int32, (C, LANE), 1)
    r2 = lax.broadcasted_iota(jnp.int32, (R, R), 0)
    c2 = lax.broadcasted_iota(jnp.int32, (R, R), 1)
    lc = int(math.log2(C))
    same = (r2 >> lc) == (c2 >> lc)
    diff = ((r2 & (C - 1)) - (c2 & (C - 1))) * sgn
    strict = same & (diff > 0)
    incl = same & (diff >= 0)
    n_sq = lc
    lh = int(math.log2(hs))

    def stack(x):
        return jnp.concatenate(
            [jnp.where((lane >> lh) == h, x, jnp.zeros_like(x)) for h in range(n_heads)], axis=0)

    def body(p, carry):
        sl = pl.ds(pl.multiple_of(p * LANE, LANE), LANE)
        a_s = stack(at_s[:, sl])
        r_s = stack(rt_s[:, sl])
        b_s = stack(bt_s[:, sl])
        k_s = stack(kt_s[:, sl])
        v_s = stack(v_ref[:, sl].astype(BF16))
        em = em_s[:, sl]
        et = et_s[:, sl]
        s0 = state[p]
        s0s = (s0 * em).astype(BF16)
        big = _dot_nt(jnp.concatenate([a_s, r_s], axis=0), jnp.concatenate([b_s, k_s], axis=0))
        zero = jnp.zeros((R, R), F32)
        lab = jnp.where(strict, big[:R, :R], zero)
        lak = jnp.where(strict, big[:R, R:], zero).astype(BF16)
        arb = jnp.where(incl, big[R:, :R], zero).astype(BF16)
        ark = jnp.where(incl, big[R:, R:], zero).astype(BF16)
        x = _dot_nt(a_s, s0s) + _dot(lak, v_s)
        for i in range(n_sq):
            lb = lab.astype(BF16)
            x = x + _dot(lb, x.astype(BF16))
            if i < n_sq - 1:
                lab = _dot(lb, lb)
        u = x.astype(BF16)
        ys = _dot_nt(r_s, s0s) + _dot(arb, u) + _dot(ark, v_s)
        y = ys[:C]
        for h in range(1, n_heads):
            y = y + ys[h * C:(h + 1) * C]
        y_ref[0, :, sl] = y
        state[p] = s0 * et + (_dot_tn(u, b_s) + _dot_tn(v_s, k_s)) * em
        return carry

    lax.fori_loop(0, n_slabs, body, 0)


def _rwkv_scan(r, v, kk, wl, b, kd):
    S, W = r.shape
    C = CHUNK
    nc = S // C
    n_slabs = W // LANE

    def shared(d, c):
        return (c + d * (nc - 1 - 2 * c), 0)

    def perdir(d, c):
        return (d, c + d * (nc - 1 - 2 * c), 0)

    s1 = pl.BlockSpec((C, W), shared)
    s2 = pl.BlockSpec((1, C, W), perdir)
    return pl.pallas_call(
        functools.partial(_scan_kernel, C=C, n_slabs=n_slabs, hs=HEAD_SIZE),
        grid=(2, nc),
        in_specs=[s1, s1, s1, s2, s2, s2],
        out_specs=s2,
        out_shape=jax.ShapeDtypeStruct((2, S, W), F32),
        scratch_shapes=[pltpu.VMEM((n_slabs, LANE, LANE), F32),
                        pltpu.VMEM((C, W), BF16), pltpu.VMEM((C, W), BF16),
                        pltpu.VMEM((C, W), BF16), pltpu.VMEM((C, W), BF16),
                        pltpu.VMEM((1, W), F32), pltpu.VMEM((1, W), F32)],
        compiler_params=_params(("arbitrary", "arbitrary")),
        name="rwkv_scan",
    )(r, v, kk, wl, b, kd)


def _fin_kernel(y_ref, g_ref, bg_ref, lg_ref, lb_ref, bd_ref, o_ref, *, hs):
    y = y_ref[0] + y_ref[1]
    bd = bd_ref[...]
    mean = _seg_sum(y, bd) * (1.0 / hs)
    yc = y - mean
    var = _seg_sum(yc * yc, bd) * (1.0 / hs)
    yn = yc * lax.rsqrt(var + GN_EPS) * lg_ref[...] + lb_ref[...]
    o_ref[...] = (yn * g_ref[...] + bg_ref[...]).astype(o_ref.dtype)


def _rwkv_finish(y2, g, bg, lw, T, cw):
    _, S, W = y2.shape
    one = pl.BlockSpec((T, cw), lambda i, j: (i, j))
    vec = pl.BlockSpec((1, cw), lambda i, j: (0, j))
    return pl.pallas_call(
        functools.partial(_fin_kernel, hs=HEAD_SIZE),
        grid=(S // T, W // cw),
        in_specs=[pl.BlockSpec((2, T, cw), lambda i, j: (0, i, j)), one, one, vec, vec,
                  pl.BlockSpec((LANE, LANE), lambda i, j: (0, 0))],
        out_specs=one,
        out_shape=jax.ShapeDtypeStruct((S, W), BF16),
        compiler_params=_params(("parallel", "parallel")),
        name="rwkv_finish",
    )(y2, g, bg, lw["lnx_g"], lw["lnx_b"], lw["bd"])


def _dft1_kernel(x_ref, cs_ref, tc_ref, ts_ref, o_ref, *, n1):
    g = _dot(cs_ref[...], x_ref[...])
    gr, gi = g[:n1], g[n1:]
    reps = x_ref.shape[1] // LANE
    tc = jnp.concatenate([tc_ref[0]] * reps, axis=1)
    ts = jnp.concatenate([ts_ref[0]] * reps, axis=1)
    o_ref[0] = (gr * tc + gi * ts).astype(o_ref.dtype)
    o_ref[1] = (gi * tc - gr * ts).astype(o_ref.dtype)


def _dft2_kernel(g_ref, m2_ref, cc_ref, sc_ref, o_ref, *, n2, gw):
    fw = g_ref.shape[-1]
    g = g_ref[...].reshape(2 * n2, fw)
    ab = _dot(m2_ref[...], g)
    a = ab[:n2].astype(BF16)
    b = ab[n2:].astype(BF16)
    cc = cc_ref[...]
    sc = sc_ref[...]
    for q in range(fw // gw):
        sl = slice(q * gw, (q + 1) * gw)
        o_ref[:, sl] = (_dot(a[:, sl], cc) + _dot(b[:, sl], sc)).astype(o_ref.dtype)


def _fourier_tables(S, gw):
    n1 = 1 << (int(math.log2(S)) // 2)
    n2 = S // n1
    assert n1 * n2 == S
    i1 = np.arange(n1)
    ang1 = 2.0 * np.pi * np.outer(i1, i1) / n1
    cs1 = np.concatenate([np.cos(ang1), -np.sin(ang1)], axis=0)
    i2 = np.arange(n2)
    angt = 2.0 * np.pi * np.outer(i2, i1) / S
    ang2 = 2.0 * np.pi * np.outer(i2, i2) / n2
    c2, s2 = np.cos(ang2), np.sin(ang2)
    m2 = np.block([[c2, s2], [-s2, c2]])
    ic = np.arange(gw)
    angc = 2.0 * np.pi * np.outer(ic, ic) / gw
    scale = 1.0 / math.sqrt(S * gw)
    return dict(n1=n1, n2=n2,
                cs1=jnp.asarray(cs1, BF16),
                tc=jnp.asarray(np.cos(angt), F32), ts=jnp.asarray(np.sin(angt), F32),
                m2=jnp.asarray(m2, BF16),
                cc=jnp.asarray(np.cos(angc) * scale, BF16), sc=jnp.asarray(np.sin(angc) * scale, BF16))


def _fourier_mixer(pf, tab):
    S, Fw = pf.shape
    n1, n2 = tab["n1"], tab["n2"]
    gw = Fw // N_FOURIER_GROUPS
    tc = jnp.broadcast_to(tab["tc"][:, :, None], (n2, n1, LANE))
    ts = jnp.broadcast_to(tab["ts"][:, :, None], (n2, n1, LANE))
    x2 = pf.reshape(n1, n2 * Fw)
    g = pl.pallas_call(
        functools.partial(_dft1_kernel, n1=n1),
        grid=(n2,),
        in_specs=[pl.BlockSpec((n1, Fw), lambda j: (0, j)),
                  pl.BlockSpec((2 * n1, n1), lambda j: (0, 0)),
                  pl.BlockSpec((1, n1, LANE), lambda j: (j, 0, 0)),
                  pl.BlockSpec((1, n1, LANE), lambda j: (j, 0, 0))],
        out_specs=pl.BlockSpec((2, n1, Fw), lambda j: (0, 0, j)),
        out_shape=jax.ShapeDtypeStruct((2, n1, n2 * Fw), BF16),
        compiler_params=_params(("parallel",)),
        name="dft_stage1",
    )(x2, tab["cs1"], tc, ts)
    g4 = g.reshape(2, n1, n2, Fw)
    out = pl.pallas_call(
        functools.partial(_dft2_kernel, n2=n2, gw=gw),
        grid=(n1,),
        in_specs=[pl.BlockSpec((2, None, n2, Fw), lambda i: (0, i, 0, 0)),
                  pl.BlockSpec((2 * n2, 2 * n2), lambda i: (0, 0)),
                  pl.BlockSpec((gw, gw), lambda i: (0, 0)),
                  pl.BlockSpec((gw, gw), lambda i: (0, 0))],
        out_specs=pl.BlockSpec((n2, Fw), lambda i: (0, i)),
        out_shape=jax.ShapeDtypeStruct((n2, n1 * Fw), BF16),
        compiler_params=_params(("parallel",)),
        name="dft_stage2",
    )(g4, tab["m2"], tab["cc"], tab["sc"])
    return out.reshape(S, Fw)


def _geglu_kernel(ug, uv, hpg, hpv, hng, hnv, cg, cv, bg, bv, o_ref):
    def conv(u_ref, hp_ref, hn_ref, c_ref, b_ref):
        u = u_ref[...]
        prev, nxt = _neighbours(u, hp_ref[0], hn_ref[0])
        c = c_ref[...]
        return c[0:1] * prev + c[1:2] * u + c[2:3] * nxt + b_ref[...]

    gate = conv(ug, hpg, hng, cg, bg)
    val = conv(uv, hpv, hnv, cv, bv)
    gelu = 0.5 * gate * (1.0 + lax.erf(gate * (1.0 / math.sqrt(2.0))))
    o_ref[...] = (gelu * val).astype(o_ref.dtype)


def _geglu(u, conv_w, conv_b, T, cw):
    S, N2 = u.shape
    dff = N2 // 2
    ncb = dff // cw
    hp, hn = _halos(u, T)

    def main(off):
        return pl.BlockSpec((T, cw), lambda i, j: (i, j + off * ncb))

    def halo(off):
        return pl.BlockSpec((1, 1, cw), lambda i, j: (i, 0, j + off * ncb))

    def vec(off, rows):
        return pl.BlockSpec((rows, cw), lambda i, j: (0, j + off * ncb))

    return pl.pallas_call(
        _geglu_kernel,
        grid=(S // T, ncb),
        in_specs=[main(0), main(1), halo(0), halo(1), halo(0), halo(1),
                  vec(0, 3), vec(1, 3), vec(0, 1), vec(1, 1)],
        out_specs=pl.BlockSpec((T, cw), lambda i, j: (i, j)),
        out_shape=jax.ShapeDtypeStruct((S, dff), BF16),
        compiler_params=_params(("parallel", "parallel")),
        name="conv_geglu",
    )(u, u, hp, hp, hn, hn, conv_w, conv_w, conv_b, conv_b)


def _layer_weights(l, W, gl, dl, al, w_in, mu_shift, decay_base, w_decay_up, aaa_base, w_aaa_up,
                   g_lora_up, k_k, k_a, r_k, lnx_g, lnx_b):
    gp = _round_up(gl, LANE)
    lp = _round_up(max(dl, al), LANE)
    o = 3 * W

    def padc(a, n):
        return jnp.pad(a, ((0, 0), (0, n - a.shape[1])))

    def padr(a, n):
        return jnp.pad(a, ((0, 0), (0, n - a.shape[1]), (0, 0)))

    wl, ml = w_in[l], mu_shift[l][None, :]
    segs = [(o, gl, gp), (o + gl, dl, lp), (o + gl + dl, dl, lp),
            (o + gl + 2 * dl, al, lp), (o + gl + 2 * dl + al, al, lp)]
    w_lora = jnp.concatenate([padc(wl[:, s:s + n], p) for s, n, p in segs], axis=1)
    mu_lora = jnp.concatenate([padc(ml[:, s:s + n], p) for s, n, p in segs], axis=1)
    n_shift = o + gl + 2 * dl + 2 * al
    hs = HEAD_SIZE
    idx = np.arange(LANE)
    bd = jnp.asarray((idx[:, None] // hs) == (idx[None, :] // hs), F32)
    return dict(
        gp=gp, lp=lp,
        w_rkv=wl[:, :o].astype(BF16), w_lora=w_lora.astype(BF16), w_f=wl[:, n_shift:].astype(BF16),
        mu_rkv=ml[:, :o], mu_lora=mu_lora,
        decay_base=decay_base[l], aaa_base=aaa_base[l],
        w_decay_up=padr(w_decay_up[l], lp).astype(BF16), w_aaa_up=padr(w_aaa_up[l], lp).astype(BF16),
        g_up=jnp.pad(g_lora_up[l], ((0, gp - gl), (0, 0))).astype(BF16),
        k_k=k_k[l][None, :], k_a=k_a[l][None, :], r_k=r_k[l].reshape(1, W),
        lnx_g=lnx_g[l][None, :], lnx_b=lnx_b[l][None, :], bd=bd)


def kernel(x, ln0_g, ln0_b, w_in, mu_shift, decay_base, w_decay_up, aaa_base, w_aaa_up, g_lora_up, k_k, k_a, r_k, lnx_g, lnx_b, w_out, ln1_g, ln1_b, w_ffn_up, ffn_conv_w, ffn_conv_b, w_ffn_down, ln2_g, ln2_b):
    B, S, D = x.shape
    depth = w_in.shape[0]
    W = k_k.shape[1]
    gl, dl, al = g_lora_up.shape[1], w_decay_up.shape[2], w_aaa_up.shape[2]
    n_shift = 3 * W + gl + 2 * dl + 2 * al
    Fw = w_in.shape[2] - n_shift
    dff = w_ffn_down.shape[1]
    alpha = float((2 * depth) ** 0.25)
    assert B == 1 and S % CHUNK == 0 and W % LANE == 0 and LANE % HEAD_SIZE == 0

    tab = _fourier_tables(S, Fw // N_FOURIER_GROUPS)
    Tp = _pick(S, 256, 8)
    cwp = _pick(W, 512)
    Tg = _pick(S, 512, 8)
    cwg = _pick(dff, 256)

    h, hb = _layer_norm(x.reshape(S, D), ln0_g, ln0_b)
    for l in range(depth):
        lw = _layer_weights(l, W, gl, dl, al, w_in, mu_shift, decay_base, w_decay_up, aaa_base,
                            w_aaa_up, g_lora_up, k_k, k_a, r_k, lnx_g, lnx_b)
        prkv = _matmul(hb, lw["w_rkv"], F32, name="in_rkv")
        plora = _matmul(hb, lw["w_lora"], F32, name="in_lora")
        pf = _matmul(hb, lw["w_f"], BF16, name="in_fourier")
        r, v, kk, wl, b, kd, g, bg = _rwkv_prep(prkv, plora, lw, Tp, cwp)
        y2 = _rwkv_scan(r, v, kk, wl, b, kd)
        y_rwkv = _rwkv_finish(y2, g, bg, lw, Tp, cwp)
        y_four = _fourier_mixer(pf, tab)
        mix = jnp.concatenate([y_rwkv, y_four], axis=1)
        att = _matmul(mix, w_out[l].astype(BF16), F32, name="out_proj")
        h, hb = _layer_norm(h, ln1_g[l], ln1_b[l], res=att, alpha=alpha)
        u = _matmul(hb, w_ffn_up[l].astype(BF16), F32, name="ffn_up")
        act = _geglu(u, ffn_conv_w[l], ffn_conv_b[l][None, :], Tg, cwg)
        ffn = _matmul(act, w_ffn_down[l].astype(BF16), F32, tm=512, tn=1024, tk=5504, name="ffn_down")
        h, hb = _layer_norm(h, ln2_g[l], ln2_b[l], res=ffn, alpha=alpha)
    return h.reshape(B, S, D)
```

```python
import functools
import math

import numpy as np
import jax
import jax.numpy as jnp
from jax import lax
from jax.experimental import pallas as pl
from jax.experimental.pallas import tpu as pltpu

F32 = jnp.float32
BF16 = jnp.bfloat16
HIGHEST = lax.Precision.HIGHEST

LANE = 128
HEAD_SIZE = 64
N_FOURIER_GROUPS = 8
LN_EPS = 1e-5
GN_EPS = 64e-5
CHUNK = 64
SCAN_UNROLL = 16
VMEM_LIMIT = 56 * 1024 * 1024


def _round_up(n, m):
    return (n + m - 1) // m * m


def _pick(dim, target, align=LANE):
    if dim <= target:
        return dim
    best = None
    for t in range(align, target + 1, align):
        if dim % t == 0:
            best = t
    assert best is not None, (dim, target, align)
    return best


def _params(sem):
    return pltpu.CompilerParams(dimension_semantics=sem, vmem_limit_bytes=VMEM_LIMIT)


def _mm_kernel(x_ref, w_ref, o_ref):
    o_ref[...] = jnp.dot(x_ref[...], w_ref[...], preferred_element_type=F32).astype(o_ref.dtype)


def _mm_acc_kernel(x_ref, w_ref, o_ref, acc_ref):
    k = pl.program_id(2)

    @pl.when(k == 0)
    def _():
        acc_ref[...] = jnp.zeros_like(acc_ref)

    acc_ref[...] += jnp.dot(x_ref[...], w_ref[...], preferred_element_type=F32)

    @pl.when(k == pl.num_programs(2) - 1)
    def _():
        o_ref[...] = acc_ref[...].astype(o_ref.dtype)


def _mm_edges_kernel(x_ref, w_ref, o_ref, f_ref, l_ref, *, tb):
    res = jnp.dot(x_ref[...], w_ref[...], preferred_element_type=F32)
    o_ref[...] = res.astype(o_ref.dtype)
    for q in range(res.shape[0] // tb):
        f_ref[q] = res[q * tb:q * tb + 1, :]
        l_ref[q] = res[(q + 1) * tb - 1:(q + 1) * tb, :]


def _matmul(x, w, out_dtype, tm=1024, tn=1024, tk=4096, name="mm", edge_rows=None):
    M, K = x.shape
    N = w.shape[1]
    tm = _pick(M, tm, 8)
    tn = _pick(N, tn)
    tk = _pick(K, tk)
    if edge_rows is not None:
        assert tk == K and tm % edge_rows == 0
        nq = tm // edge_rows
        edge = pl.BlockSpec((nq, 1, tn), lambda j, i: (i, 0, j))
        esd = jax.ShapeDtypeStruct((M // edge_rows, 1, N), F32)
        return pl.pallas_call(
            functools.partial(_mm_edges_kernel, tb=edge_rows),
            grid=(N // tn, M // tm),
            in_specs=[pl.BlockSpec((tm, K), lambda j, i: (i, 0)),
                      pl.BlockSpec((K, tn), lambda j, i: (0, j))],
            out_specs=(pl.BlockSpec((tm, tn), lambda j, i: (i, j)), edge, edge),
            out_shape=(jax.ShapeDtypeStruct((M, N), out_dtype), esd, esd),
            compiler_params=_params(("parallel", "parallel")),
            name=name,
        )(x, w)
    if tk == K:
        return pl.pallas_call(
            _mm_kernel,
            grid=(N // tn, M // tm),
            in_specs=[pl.BlockSpec((tm, K), lambda j, i: (i, 0)),
                      pl.BlockSpec((K, tn), lambda j, i: (0, j))],
            out_specs=pl.BlockSpec((tm, tn), lambda j, i: (i, j)),
            out_shape=jax.ShapeDtypeStruct((M, N), out_dtype),
            compiler_params=_params(("parallel", "parallel")),
            name=name,
        )(x, w)
    return pl.pallas_call(
        _mm_acc_kernel,
        grid=(N // tn, M // tm, K // tk),
        in_specs=[pl.BlockSpec((tm, tk), lambda j, i, k: (i, k)),
                  pl.BlockSpec((tk, tn), lambda j, i, k: (k, j))],
        out_specs=pl.BlockSpec((tm, tn), lambda j, i, k: (i, j)),
        out_shape=jax.ShapeDtypeStruct((M, N), out_dtype),
        scratch_shapes=[pltpu.VMEM((tm, tn), F32)],
        compiler_params=_params(("parallel", "parallel", "arbitrary")),
        name=name,
    )(x, w)


def _ln_rows(x, g, b):
    mu = jnp.mean(x, axis=-1, keepdims=True)
    xc = x - mu
    var = jnp.mean(xc * xc, axis=-1, keepdims=True)
    return xc * lax.rsqrt(var + LN_EPS) * g + b


def _ln_kernel(x_ref, g_ref, b_ref, o_ref, ob_ref):
    y = _ln_rows(x_ref[...], g_ref[...], b_ref[...])
    o_ref[...] = y
    ob_ref[...] = y.astype(BF16)


def _layer_norm(x, g, b):
    M, D = x.shape
    tm = _pick(M, 256, 8)
    row = pl.BlockSpec((tm, D), lambda i: (i, 0))
    vec = pl.BlockSpec((1, D), lambda i: (0, 0))
    return pl.pallas_call(
        _ln_kernel, grid=(M // tm,), in_specs=[row, vec, vec], out_specs=(row, row),
        out_shape=(jax.ShapeDtypeStruct((M, D), F32), jax.ShapeDtypeStruct((M, D), BF16)),
        compiler_params=_params(("parallel",)), name="ln",
    )(x, g.reshape(1, D), b.reshape(1, D))


def _halo_specs(T_blocks, cw, col):
    prev = pl.BlockSpec((1, 1, cw), lambda i, j: (jnp.maximum(i - 1, 0), 0, col(j)))
    nxt = pl.BlockSpec((1, 1, cw), lambda i, j: (jnp.minimum(i + 1, T_blocks - 1), 0, col(j)))
    return prev, nxt


def _neighbours(z, last_ref, first_ref):
    T = z.shape[0]
    i = pl.program_id(0)
    zp = jnp.where(i > 0, last_ref[0], 0.0)
    zn = jnp.where(i < pl.num_programs(0) - 1, first_ref[0], 0.0)
    row = lax.broadcasted_iota(jnp.int32, (T, 1), 0)
    prev = jnp.where(row == 0, zp, pltpu.roll(z, 1, 0))
    nxt = jnp.where(row == T - 1, zn, pltpu.roll(z, T - 1, 0))
    return prev, nxt


def _token_shift(z, last_ref, first_ref, mu):
    prev, nxt = _neighbours(z, last_ref, first_ref)
    return z + (0.5 * (prev + nxt) - z) * mu


def _seg_sum(x, bd):
    parts = [jnp.dot(x[:, i * LANE:(i + 1) * LANE], bd, precision=HIGHEST, preferred_element_type=F32)
             for i in range(x.shape[1] // LANE)]
    return parts[0] if len(parts) == 1 else jnp.concatenate(parts, axis=1)


def _sigmoid(x):
    return 1.0 / (1.0 + jnp.exp(-x))


def _softplus(x):
    return jnp.maximum(x, 0.0) + jnp.log1p(jnp.exp(-jnp.abs(x)))


def _prep_kernel(pr, pk, pv, hpr, hpk, hpv, hnr, hnk, hnv, plo, hpl, hnl,
                 mur, muk, muv, mul, dbase, wdu, abase, wau, gup, kkw, kaw, rkw, bd,
                 r_o, v_o, kk_o, wl_o, b_o, kd_o, g_o, bg_o, *, gp, lp):
    r = _token_shift(pr[...], hpr, hnr, mur[...])
    k = _token_shift(pk[...], hpk, hnk, muk[...])
    v = _token_shift(pv[...], hpv, hnv, muv[...])
    zl = _token_shift(plo[...], hpl, hnl, mul[...])
    bdm = bd[...]

    g = jnp.dot(_sigmoid(zl[:, :gp]).astype(BF16), gup[...], preferred_element_type=F32)
    kk = k * kkw[...]
    kk = kk / jnp.maximum(jnp.sqrt(_seg_sum(kk * kk, bdm)), 1e-12)
    ka = kaw[...]
    ksum = jnp.zeros_like(k)
    for d in range(2):
        wd = jnp.tanh(zl[:, gp + d * lp:gp + (d + 1) * lp]).astype(BF16)
        x = dbase[d:d + 1, :] + jnp.dot(wd, wdu[d], preferred_element_type=F32)
        wl_o[d] = -jnp.exp(-_softplus(-x) - 0.5)
        ad = zl[:, gp + (2 + d) * lp:gp + (3 + d) * lp].astype(BF16)
        a = _sigmoid(abase[d:d + 1, :] + jnp.dot(ad, wau[d], preferred_element_type=F32))
        kd = k * (1.0 + (a - 1.0) * ka)
        kd_o[d] = kd
        b_o[d] = a * kk
        ksum = ksum + kd
    bonus = _seg_sum(r * ksum * rkw[...], bdm) * v
    r_o[...] = r
    v_o[...] = v
    kk_o[...] = kk
    g_o[...] = g
    bg_o[...] = bonus * g


def _rwkv_prep(prkv, rkv_first, rkv_last, plora, lora_first, lora_last, lw, T, cw):
    S = prkv.shape[0]
    W = prkv.shape[1] // 3
    LW = plora.shape[1]
    gp, lp = lw["gp"], lw["lp"]
    nb, ncb = S // T, W // cw

    def main(off):
        return pl.BlockSpec((T, cw), lambda i, j: (i, j + off * ncb))

    def vec(off, rows=1):
        return pl.BlockSpec((rows, cw), lambda i, j: (0, j + off * ncb))

    hp = [_halo_specs(nb, cw, lambda j, off=off: j + off * ncb) for off in range(3)]
    hpl, hnl = _halo_specs(nb, LW, lambda j: 0)
    in_specs = [main(0), main(1), main(2), hp[0][0], hp[1][0], hp[2][0], hp[0][1], hp[1][1], hp[2][1],
                pl.BlockSpec((T, LW), lambda i, j: (i, 0)), hpl, hnl,
                vec(0), vec(1), vec(2),
                pl.BlockSpec((1, LW), lambda i, j: (0, 0)),
                vec(0, 2),
                pl.BlockSpec((2, lp, cw), lambda i, j: (0, 0, j)),
                vec(0, 2),
                pl.BlockSpec((2, lp, cw), lambda i, j: (0, 0, j)),
                pl.BlockSpec((gp, cw), lambda i, j: (0, j)),
                vec(0), vec(0), vec(0),
                pl.BlockSpec((LANE, LANE), lambda i, j: (0, 0))]
    one = pl.BlockSpec((T, cw), lambda i, j: (i, j))
    two = pl.BlockSpec((2, T, cw), lambda i, j: (0, i, j))
    sd1 = jax.ShapeDtypeStruct((S, W), F32)
    sd2 = jax.ShapeDtypeStruct((2, S, W), F32)
    return pl.pallas_call(
        functools.partial(_prep_kernel, gp=gp, lp=lp),
        grid=(nb, ncb),
        in_specs=in_specs,
        out_specs=(one, one, one, two, two, two, one, one),
        out_shape=(sd1, sd1, sd1, sd2, sd2, sd2, sd1, sd1),
        compiler_params=_params(("parallel", "parallel")),
        name="rwkv_prep",
    )(prkv, prkv, prkv, rkv_last, rkv_last, rkv_last, rkv_first, rkv_first, rkv_first,
      plora, lora_last, lora_first,
      lw["mu_rkv"], lw["mu_rkv"], lw["mu_rkv"], lw["mu_lora"], lw["decay_base"], lw["w_decay_up"],
      lw["aaa_base"], lw["w_aaa_up"], lw["g_up"], lw["k_k"], lw["k_a"], lw["r_k"], lw["bd"])


def _dot_nt(a, b):
    return lax.dot_general(a, b, (((1,), (1,)), ((), ())), preferred_element_type=F32)


def _dot_tn(a, b):
    return lax.dot_general(a, b, (((0,), (0,)), ((), ())), preferred_element_type=F32)


def _dot(a, b):
    return jnp.dot(a, b, preferred_element_type=F32)


def _scan_kernel(r_ref, v_ref, kk_ref, wl_ref, b_ref, kd_ref, y_ref,
                 state, *, C, n_slabs, hs, unroll):
    d = pl.program_id(0)
    c = pl.program_id(1)

    @pl.when(c == 0)
    def _():
        state[...] = jnp.zeros_like(state)

    sgn = 1 - 2 * d
    ri = lax.broadcasted_iota(jnp.int32, (C, C), 0)
    ci = lax.broadcasted_iota(jnp.int32, (C, C), 1)
    tri = ((ri - ci) * sgn >= 0).astype(BF16)

    def split3(w):
        w1 = w.astype(BF16)
        res = w - w1.astype(F32)
        w2 = res.astype(BF16)
        w3 = (res - w2.astype(F32)).astype(BF16)
        return jnp.concatenate([w1, w2, w3], axis=1)

    n_heads = LANE // hs
    R = n_heads * C
    lc = int(math.log2(C))
    lh = int(math.log2(hs))
    n_sq = lc
    ri = lax.broadcasted_iota(jnp.int32, (C, 2 * R), 0)
    ci = lax.broadcasted_iota(jnp.int32, (C, 2 * R), 1)
    diff = (ri - (ci & (C - 1))) * sgn
    strict = diff > 0
    incl = diff >= 0
    sr = lax.broadcasted_iota(jnp.int32, (LANE, LANE), 0)
    sc = lax.broadcasted_iota(jnp.int32, (LANE, LANE), 1)
    same_head = (sr >> lh) == (sc >> lh)

    def stack(x, shift):
        grp = lax.broadcasted_iota(jnp.int32, x.shape, 1) >> shift
        return jnp.concatenate(
            [jnp.where(grp == h, x, jnp.zeros_like(x)) for h in range(n_heads)], axis=0)

    def lanes(p):
        return pl.ds(pl.multiple_of(p * LANE, LANE), LANE)

    def body(q, carry):
        ps = [q * unroll + j for j in range(unroll)]
        n = range(unroll)
        wl = [wl_ref[0, :, lanes(p)] for p in ps]
        s0 = [state[p] for p in ps]
        v = [v_ref[:, lanes(p)].astype(BF16) for p in ps]
        cum3 = [_dot(tri, split3(wl[j])) for j in n]
        cum = [cum3[j][:, :LANE] + cum3[j][:, LANE:2 * LANE] + cum3[j][:, 2 * LANE:] for j in n]
        tot = [jnp.sum(wl[j], axis=0, keepdims=True) for j in n]
        m = [0.5 * tot[j] for j in n]
        e_iv = [jnp.exp(m[j] - cum[j]) for j in n]
        at = [(-kk_ref[:, lanes(ps[j])] * jnp.exp(cum[j] - wl[j] - m[j])).astype(BF16) for j in n]
        rt = [(r_ref[:, lanes(ps[j])] * jnp.exp(cum[j] - m[j])).astype(BF16) for j in n]
        bt = [(b_ref[0, :, lanes(ps[j])] * e_iv[j]).astype(BF16) for j in n]
        kt = [(kd_ref[0, :, lanes(ps[j])] * e_iv[j]).astype(BF16) for j in n]
        em = [jnp.exp(m[j]) for j in n]
        et = [jnp.exp(tot[j]) for j in n]
        s0s = [(s0[j] * em[j]).astype(BF16) for j in n]
        ar = [jnp.concatenate([at[j], rt[j]], axis=0) for j in n]
        big = [_dot_nt(ar[j], jnp.concatenate([stack(bt[j], lh), stack(kt[j], lh)], axis=0))
               for j in n]
        sx = [_dot_nt(ar[j], s0s[j]) for j in n]
        top = [jnp.where(strict, big[j][:C], jnp.zeros((C, 2 * R), F32)) for j in n]
        lab = [top[j][:, :R].astype(BF16) for j in n]
        lak = [top[j][:, R:].astype(BF16) for j in n]
        arbk = [jnp.where(incl, big[j][C:], jnp.zeros((C, 2 * R), F32)).astype(BF16) for j in n]
        v_s = [stack(v[j], lh) for j in n]
        x = [sx[j][:C] + _dot(lak[j], v_s[j]) for j in n]
        for i in range(n_sq - 1):
            both = [_dot(lab[j], jnp.concatenate([stack(x[j].astype(BF16), lh), stack(lab[j], lc)], axis=1))
                    for j in n]
            x = [x[j] + both[j][:, :LANE] for j in n]
            lab = [both[j][:, LANE:].astype(BF16) for j in n]
        u = [(x[j] + _dot(lab[j], stack(x[j].astype(BF16), lh))).astype(BF16) for j in n]
        y = [sx[j][C:] + _dot(arbk[j], jnp.concatenate([stack(u[j], lh), v_s[j]], axis=0)) for j in n]
        upd = [_dot_tn(jnp.concatenate([u[j], v[j]], axis=0), jnp.concatenate([bt[j], kt[j]], axis=0))
               for j in n]
        for j in n:
            y_ref[0, :, lanes(ps[j])] = y[j]
            state[ps[j]] = s0[j] * et[j] + jnp.where(same_head, upd[j], jnp.zeros_like(upd[j])) * em[j]
        return carry

    lax.fori_loop(0, n_slabs // unroll, body, 0)


def _rwkv_scan(r, v, kk, wl, b, kd):
    S, W = r.shape
    C = CHUNK
    nc = S // C
    n_slabs = W // LANE

    def shared(d, c):
        return (c + d * (nc - 1 - 2 * c), 0)

    def perdir(d, c):
        return (d, c + d * (nc - 1 - 2 * c), 0)

    s1 = pl.BlockSpec((C, W), shared)
    s2 = pl.BlockSpec((1, C, W), perdir)
    return pl.pallas_call(
        functools.partial(_scan_kernel, C=C, n_slabs=n_slabs, hs=HEAD_SIZE,
                          unroll=math.gcd(n_slabs, SCAN_UNROLL)),
        grid=(2, nc),
        in_specs=[s1, s1, s1, s2, s2, s2],
        out_specs=s2,
        out_shape=jax.ShapeDtypeStruct((2, S, W), F32),
        scratch_shapes=[pltpu.VMEM((n_slabs, LANE, LANE), F32)],
        compiler_params=_params(("arbitrary", "arbitrary")),
        name="rwkv_scan",
    )(r, v, kk, wl, b, kd)


def _fin_kernel(y_ref, g_ref, bg_ref, lg_ref, lb_ref, bd_ref, o_ref, *, hs):
    y = y_ref[0] + y_ref[1]
    bd = bd_ref[...]
    mean = _seg_sum(y, bd) * (1.0 / hs)
    yc = y - mean
    var = _seg_sum(yc * yc, bd) * (1.0 / hs)
    yn = yc * lax.rsqrt(var + GN_EPS) * lg_ref[...] + lb_ref[...]
    o_ref[...] = (yn * g_ref[...] + bg_ref[...]).astype(o_ref.dtype)


def _rwkv_finish(y2, g, bg, lw, T, cw):
    _, S, W = y2.shape
    one = pl.BlockSpec((T, cw), lambda i, j: (i, j))
    vec = pl.BlockSpec((1, cw), lambda i, j: (0, j))
    return pl.pallas_call(
        functools.partial(_fin_kernel, hs=HEAD_SIZE),
        grid=(S // T, W // cw),
        in_specs=[pl.BlockSpec((2, T, cw), lambda i, j: (0, i, j)), one, one, vec, vec,
                  pl.BlockSpec((LANE, LANE), lambda i, j: (0, 0))],
        out_specs=one,
        out_shape=jax.ShapeDtypeStruct((S, W), BF16),
        compiler_params=_params(("parallel", "parallel")),
        name="rwkv_finish",
    )(y2, g, bg, lw["lnx_g"], lw["lnx_b"], lw["bd"])


def _dft1_kernel(x_ref, cs_ref, tc_ref, ts_ref, o_ref, *, n1):
    g = _dot(cs_ref[...], x_ref[...])
    gr, gi = g[:n1], g[n1:]
    reps = x_ref.shape[1] // LANE
    tc = jnp.concatenate([tc_ref[0]] * reps, axis=1)
    ts = jnp.concatenate([ts_ref[0]] * reps, axis=1)
    o_ref[0] = (gr * tc + gi * ts).astype(o_ref.dtype)
    o_ref[1] = (gi * tc - gr * ts).astype(o_ref.dtype)


def _dft2_kernel(g_ref, m2_ref, cc_ref, sc_ref, o_ref, *, n2, gw):
    fw = g_ref.shape[-1]
    g = g_ref[...].reshape(2 * n2, fw)
    ab = _dot(m2_ref[...], g)
    a = ab[:n2].astype(BF16)
    b = ab[n2:].astype(BF16)
    cc = cc_ref[...]
    sc = sc_ref[...]
    for q in range(fw // gw):
        sl = slice(q * gw, (q + 1) * gw)
        o_ref[:, sl] = (_dot(a[:, sl], cc) + _dot(b[:, sl], sc)).astype(o_ref.dtype)


def _fourier_tables(S, gw):
    n1 = 1 << (int(math.log2(S)) // 2)
    n2 = S // n1
    assert n1 * n2 == S
    i1 = np.arange(n1)
    ang1 = 2.0 * np.pi * np.outer(i1, i1) / n1
    cs1 = np.concatenate([np.cos(ang1), -np.sin(ang1)], axis=0)
    i2 = np.arange(n2)
    angt = 2.0 * np.pi * np.outer(i2, i1) / S
    ang2 = 2.0 * np.pi * np.outer(i2, i2) / n2
    c2, s2 = np.cos(ang2), np.sin(ang2)
    m2 = np.block([[c2, s2], [-s2, c2]])
    ic = np.arange(gw)
    angc = 2.0 * np.pi * np.outer(ic, ic) / gw
    scale = 1.0 / math.sqrt(S * gw)
    return dict(n1=n1, n2=n2,
                cs1=jnp.asarray(cs1, BF16),
                tc=jnp.asarray(np.cos(angt), F32), ts=jnp.asarray(np.sin(angt), F32),
                m2=jnp.asarray(m2, BF16),
                cc=jnp.asarray(np.cos(angc) * scale, BF16), sc=jnp.asarray(np.sin(angc) * scale, BF16))


def _fourier_mixer(pf, tab):
    S, Fw = pf.shape
    n1, n2 = tab["n1"], tab["n2"]
    gw = Fw // N_FOURIER_GROUPS
    tc = jnp.broadcast_to(tab["tc"][:, :, None], (n2, n1, LANE))
    ts = jnp.broadcast_to(tab["ts"][:, :, None], (n2, n1, LANE))
    x2 = pf.reshape(n1, n2 * Fw)
    g = pl.pallas_call(
        functools.partial(_dft1_kernel, n1=n1),
        grid=(n2,),
        in_specs=[pl.BlockSpec((n1, Fw), lambda j: (0, j)),
                  pl.BlockSpec((2 * n1, n1), lambda j: (0, 0)),
                  pl.BlockSpec((1, n1, LANE), lambda j: (j, 0, 0)),
                  pl.BlockSpec((1, n1, LANE), lambda j: (j, 0, 0))],
        out_specs=pl.BlockSpec((2, n1, Fw), lambda j: (0, 0, j)),
        out_shape=jax.ShapeDtypeStruct((2, n1, n2 * Fw), BF16),
        compiler_params=_params(("parallel",)),
        name="dft_stage1",
    )(x2, tab["cs1"], tc, ts)
    g4 = g.reshape(2, n1, n2, Fw)
    out = pl.pallas_call(
        functools.partial(_dft2_kernel, n2=n2, gw=gw),
        grid=(n1,),
        in_specs=[pl.BlockSpec((2, None, n2, Fw), lambda i: (0, i, 0, 0)),
                  pl.BlockSpec((2 * n2, 2 * n2), lambda i: (0, 0)),
                  pl.BlockSpec((gw, gw), lambda i: (0, 0)),
                  pl.BlockSpec((gw, gw), lambda i: (0, 0))],
        out_specs=pl.BlockSpec((n2, Fw), lambda i: (0, i)),
        out_shape=jax.ShapeDtypeStruct((n2, n1 * Fw), BF16),
        compiler_params=_params(("parallel",)),
        name="dft_stage2",
    )(g4, tab["m2"], tab["cc"], tab["sc"])
    return out.reshape(S, Fw)


def _acc_dot(o_ref, x, w_ref):
    D = o_ref.shape[1]
    pn = _pick(D, 512)
    for n in range(0, D, pn):
        o_ref[:, n:n + pn] += _dot(x, w_ref[:, n:n + pn])


def _res_ln_finish(k, h_ref, g_ref, b_ref, o_ref, ob_ref, alpha):
    @pl.when(k == pl.num_programs(1) - 1)
    def _():
        T = o_ref.shape[0]
        pr = _pick(T, 64, 16)
        for r in range(0, T, pr):
            rows = pl.ds(r, pr)
            y = _ln_rows(alpha * h_ref[rows, :] + o_ref[rows, :], g_ref[...], b_ref[...])
            o_ref[rows, :] = y
            ob_ref[rows, :] = y.astype(BF16)


def _proj_ln_kernel(x_ref, w_ref, h_ref, g_ref, b_ref, o_ref, ob_ref, *, alpha):
    k = pl.program_id(1)

    @pl.when(k == 0)
    def _():
        o_ref[...] = jnp.zeros_like(o_ref)

    _acc_dot(o_ref, x_ref[...], w_ref)
    _res_ln_finish(k, h_ref, g_ref, b_ref, o_ref, ob_ref, alpha)


def _ffn_tail_kernel(ug, uv, lg, lv, fg, fv, cg, cv, bg, bv, w_ref, h_ref, g_ref, b_ref, o_ref, ob_ref,
                     *, alpha):
    k = pl.program_id(1)

    @pl.when(k == 0)
    def _():
        o_ref[...] = jnp.zeros_like(o_ref)

    def conv(u_ref, last_ref, first_ref, c_ref, cb_ref):
        u = u_ref[...]
        prev, nxt = _neighbours(u, last_ref, first_ref)
        c = c_ref[...]
        return c[0:1] * prev + c[1:2] * u + c[2:3] * nxt + cb_ref[...]

    gate = conv(ug, lg, fg, cg, bg)
    val = conv(uv, lv, fv, cv, bv)
    gelu = 0.5 * gate * (1.0 + lax.erf(gate * (1.0 / math.sqrt(2.0))))
    _acc_dot(o_ref, (gelu * val).astype(BF16), w_ref)
    _res_ln_finish(k, h_ref, g_ref, b_ref, o_ref, ob_ref, alpha)


def _row_tile_specs(T, D):
    h_spec = pl.BlockSpec((T, D), lambda i, k: (i, 0), pipeline_mode=pl.Buffered(1))
    vec = pl.BlockSpec((1, D), lambda i, k: (0, 0))
    out = pl.BlockSpec((T, D), lambda i, k: (i, 0))
    return h_spec, vec, out


def _proj_ln(x, w, h, g, b, alpha, T=512, tk=512, name="proj_ln"):
    M, K = x.shape
    D = w.shape[1]
    T = _pick(M, T, 8)
    tk = _pick(K, tk)
    h_spec, vec, out = _row_tile_specs(T, D)
    return pl.pallas_call(
        functools.partial(_proj_ln_kernel, alpha=alpha),
        grid=(M // T, K // tk),
        in_specs=[pl.BlockSpec((T, tk), lambda i, k: (i, k)),
                  pl.BlockSpec((tk, D), lambda i, k: (k, 0)), h_spec, vec, vec],
        out_specs=(out, out),
        out_shape=(jax.ShapeDtypeStruct((M, D), F32), jax.ShapeDtypeStruct((M, D), BF16)),
        compiler_params=_params(("parallel", "arbitrary")),
        name=name,
    )(x, w, h, g.reshape(1, D), b.reshape(1, D))


def _ffn_tail(u, u_first, u_last, conv_w, conv_b, w_down, h, g, b, alpha, T, tk=512):
    S, N2 = u.shape
    dffp = N2 // 2
    D = w_down.shape[1]
    tk = _pick(dffp, tk)
    nk = dffp // tk
    nb = S // T
    h_spec, vec, out = _row_tile_specs(T, D)

    def main(off):
        return pl.BlockSpec((T, tk), lambda i, k: (i, k + off * nk))

    def cvec(off, rows):
        return pl.BlockSpec((rows, tk), lambda i, k: (0, k + off * nk))

    hg = _halo_specs(nb, tk, lambda k: k)
    hv = _halo_specs(nb, tk, lambda k: k + nk)
    return pl.pallas_call(
        functools.partial(_ffn_tail_kernel, alpha=alpha),
        grid=(nb, nk),
        in_specs=[main(0), main(1), hg[0], hv[0], hg[1], hv[1],
                  cvec(0, 3), cvec(1, 3), cvec(0, 1), cvec(1, 1),
                  pl.BlockSpec((tk, D), lambda i, k: (k, 0)), h_spec, vec, vec],
        out_specs=(out, out),
        out_shape=(jax.ShapeDtypeStruct((S, D), F32), jax.ShapeDtypeStruct((S, D), BF16)),
        compiler_params=_params(("parallel", "arbitrary")),
        name="ffn_tail",
    )(u, u, u_last, u_last, u_first, u_first, conv_w, conv_w, conv_b, conv_b, w_down, h,
      g.reshape(1, D), b.reshape(1, D))


def _layer_weights(l, W, gl, dl, al, w_in, mu_shift, decay_base, w_decay_up, aaa_base, w_aaa_up,
                   g_lora_up, k_k, k_a, r_k, lnx_g, lnx_b):
    gp = _round_up(gl, LANE)
    lp = _round_up(max(dl, al), LANE)
    o = 3 * W

    def padc(a, n):
        return jnp.pad(a, ((0, 0), (0, n - a.shape[1])))

    def padr(a, n):
        return jnp.pad(a, ((0, 0), (0, n - a.shape[1]), (0, 0)))

    wl, ml = w_in[l], mu_shift[l][None, :]
    segs = [(o, gl, gp), (o + gl, dl, lp), (o + gl + dl, dl, lp),
            (o + gl + 2 * dl, al, lp), (o + gl + 2 * dl + al, al, lp)]
    w_lora = jnp.concatenate([padc(wl[:, s:s + n], p) for s, n, p in segs], axis=1)
    mu_lora = jnp.concatenate([padc(ml[:, s:s + n], p) for s, n, p in segs], axis=1)
    n_shift = o + gl + 2 * dl + 2 * al
    hs = HEAD_SIZE
    idx = np.arange(LANE)
    bd = jnp.asarray((idx[:, None] // hs) == (idx[None, :] // hs), F32)
    return dict(
        gp=gp, lp=lp,
        w_rkv=wl[:, :o].astype(BF16), w_lora=w_lora.astype(BF16), w_f=wl[:, n_shift:].astype(BF16),
        mu_rkv=ml[:, :o], mu_lora=mu_lora,
        decay_base=decay_base[l], aaa_base=aaa_base[l],
        w_decay_up=padr(w_decay_up[l], lp).astype(BF16), w_aaa_up=padr(w_aaa_up[l], lp).astype(BF16),
        g_up=jnp.pad(g_lora_up[l], ((0, gp - gl), (0, 0))).astype(BF16),
        k_k=k_k[l][None, :], k_a=k_a[l][None, :], r_k=r_k[l].reshape(1, W),
        lnx_g=lnx_g[l][None, :], lnx_b=lnx_b[l][None, :], bd=bd)


def kernel(x, ln0_g, ln0_b, w_in, mu_shift, decay_base, w_decay_up, aaa_base, w_aaa_up, g_lora_up, k_k, k_a, r_k, lnx_g, lnx_b, w_out, ln1_g, ln1_b, w_ffn_up, ffn_conv_w, ffn_conv_b, w_ffn_down, ln2_g, ln2_b):
    B, S, D = x.shape
    depth = w_in.shape[0]
    W = k_k.shape[1]
    gl, dl, al = g_lora_up.shape[1], w_decay_up.shape[2], w_aaa_up.shape[2]
    n_shift = 3 * W + gl + 2 * dl + 2 * al
    Fw = w_in.shape[2] - n_shift
    dff = w_ffn_down.shape[1]
    alpha = float((2 * depth) ** 0.25)
    assert B == 1 and S % CHUNK == 0 and W % LANE == 0 and LANE % HEAD_SIZE == 0

    tab = _fourier_tables(S, Fw // N_FOURIER_GROUPS)
    Tp = _pick(S, 256, 8)
    cwp = _pick(W, 512)
    Tf = _pick(S, 512, 8)
    dffp = _round_up(dff, 512)
    pad = dffp - dff

    def pad_gv(a):
        cfg = [(0, 0)] * (a.ndim - 1) + [(0, pad)]
        return jnp.concatenate([jnp.pad(a[..., :dff], cfg), jnp.pad(a[..., dff:], cfg)], axis=-1)

    h, hb = _layer_norm(x.reshape(S, D), ln0_g, ln0_b)
    for l in range(depth):
        lw = _layer_weights(l, W, gl, dl, al, w_in, mu_shift, decay_base, w_decay_up, aaa_base,
                            w_aaa_up, g_lora_up, k_k, k_a, r_k, lnx_g, lnx_b)
        prkv, rkv_first, rkv_last = _matmul(hb, lw["w_rkv"], F32, name="in_rkv", edge_rows=Tp)
        plora, lora_first, lora_last = _matmul(hb, lw["w_lora"], F32, name="in_lora", edge_rows=Tp)
        pf = _matmul(hb, lw["w_f"], BF16, name="in_fourier")
        r, v, kk, wl, b, kd, g, bg = _rwkv_prep(prkv, rkv_first, rkv_last, plora, lora_first, lora_last,
                                                lw, Tp, cwp)
        y2 = _rwkv_scan(r, v, kk, wl, b, kd)
        y_rwkv = _rwkv_finish(y2, g, bg, lw, Tp, cwp)
        y_four = _fourier_mixer(pf, tab)
        mix = jnp.concatenate([y_rwkv, y_four], axis=1)
        h, hb = _proj_ln(mix, w_out[l].astype(BF16), h, ln1_g[l], ln1_b[l], alpha, T=Tf, name="out_proj_ln")
        u, u_first, u_last = _matmul(hb, pad_gv(w_ffn_up[l]).astype(BF16), F32, name="ffn_up", edge_rows=Tf)
        w_down = jnp.pad(w_ffn_down[l], ((0, pad), (0, 0))).astype(BF16)
        h, hb = _ffn_tail(u, u_first, u_last, pad_gv(ffn_conv_w[l]), pad_gv(ffn_conv_b[l][None, :]),
                          w_down, h, ln2_g[l], ln2_b[l], alpha, Tf)
    return h.reshape(B, S, D)
```

```python
import functools
import math

import numpy as np
import jax
import jax.numpy as jnp
from jax import lax
from jax.experimental import pallas as pl
from jax.experimental.pallas import tpu as pltpu

F32 = jnp.float32
BF16 = jnp.bfloat16

LANE = 128
HEAD_SIZE = 64
N_FOURIER_GROUPS = 8
LN_EPS = 1e-5
GN_EPS = 64e-5
CHUNK = 64
SCAN_UNROLL = 16
VMEM_LIMIT = 56 * 1024 * 1024


def _round_up(n, m):
    return (n + m - 1) // m * m


def _pick(dim, target, align=LANE):
    if dim <= target:
        return dim
    best = None
    for t in range(align, target + 1, align):
        if dim % t == 0:
            best = t
    assert best is not None, (dim, target, align)
    return best


def _params(sem):
    return pltpu.CompilerParams(dimension_semantics=sem, vmem_limit_bytes=VMEM_LIMIT)


def _mm_kernel(x_ref, w_ref, o_ref):
    o_ref[...] = jnp.dot(x_ref[...], w_ref[...], preferred_element_type=F32).astype(o_ref.dtype)


def _mm_acc_kernel(x_ref, w_ref, o_ref, acc_ref):
    k = pl.program_id(2)

    @pl.when(k == 0)
    def _():
        acc_ref[...] = jnp.zeros_like(acc_ref)

    acc_ref[...] += jnp.dot(x_ref[...], w_ref[...], preferred_element_type=F32)

    @pl.when(k == pl.num_programs(2) - 1)
    def _():
        o_ref[...] = acc_ref[...].astype(o_ref.dtype)


def _mm_edges_kernel(x_ref, w_ref, o_ref, f_ref, l_ref, *, tb):
    res = jnp.dot(x_ref[...], w_ref[...], preferred_element_type=F32)
    o_ref[...] = res.astype(o_ref.dtype)
    for q in range(res.shape[0] // tb):
        f_ref[q] = res[q * tb:q * tb + 1, :]
        l_ref[q] = res[(q + 1) * tb - 1:(q + 1) * tb, :]


def _matmul(x, w, out_dtype, tm=1024, tn=1024, tk=4096, name="mm", edge_rows=None):
    M, K = x.shape
    N = w.shape[1]
    tm = _pick(M, tm, 8)
    tn = _pick(N, tn)
    tk = _pick(K, tk)
    if edge_rows is not None:
        assert tk == K and tm % edge_rows == 0
        nq = tm // edge_rows
        edge = pl.BlockSpec((nq, 1, tn), lambda j, i: (i, 0, j))
        esd = jax.ShapeDtypeStruct((M // edge_rows, 1, N), F32)
        return pl.pallas_call(
            functools.partial(_mm_edges_kernel, tb=edge_rows),
            grid=(N // tn, M // tm),
            in_specs=[pl.BlockSpec((tm, K), lambda j, i: (i, 0)),
                      pl.BlockSpec((K, tn), lambda j, i: (0, j))],
            out_specs=(pl.BlockSpec((tm, tn), lambda j, i: (i, j)), edge, edge),
            out_shape=(jax.ShapeDtypeStruct((M, N), out_dtype), esd, esd),
            compiler_params=_params(("parallel", "parallel")),
            name=name,
        )(x, w)
    if tk == K:
        return pl.pallas_call(
            _mm_kernel,
            grid=(N // tn, M // tm),
            in_specs=[pl.BlockSpec((tm, K), lambda j, i: (i, 0)),
                      pl.BlockSpec((K, tn), lambda j, i: (0, j))],
            out_specs=pl.BlockSpec((tm, tn), lambda j, i: (i, j)),
            out_shape=jax.ShapeDtypeStruct((M, N), out_dtype),
            compiler_params=_params(("parallel", "parallel")),
            name=name,
        )(x, w)
    return pl.pallas_call(
        _mm_acc_kernel,
        grid=(N // tn, M // tm, K // tk),
        in_specs=[pl.BlockSpec((tm, tk), lambda j, i, k: (i, k)),
                  pl.BlockSpec((tk, tn), lambda j, i, k: (k, j))],
        out_specs=pl.BlockSpec((tm, tn), lambda j, i, k: (i, j)),
        out_shape=jax.ShapeDtypeStruct((M, N), out_dtype),
        scratch_shapes=[pltpu.VMEM((tm, tn), F32)],
        compiler_params=_params(("parallel", "parallel", "arbitrary")),
        name=name,
    )(x, w)


def _ln_rows(x, g, b):
    mu = jnp.mean(x, axis=-1, keepdims=True)
    xc = x - mu
    var = jnp.mean(xc * xc, axis=-1, keepdims=True)
    return xc * lax.rsqrt(var + LN_EPS) * g + b


def _ln_kernel(x_ref, g_ref, b_ref, o_ref, ob_ref):
    y = _ln_rows(x_ref[...], g_ref[...], b_ref[...])
    o_ref[...] = y
    ob_ref[...] = y.astype(BF16)


def _layer_norm(x, g, b):
    M, D = x.shape
    tm = _pick(M, 256, 8)
    row = pl.BlockSpec((tm, D), lambda i: (i, 0))
    vec = pl.BlockSpec((1, D), lambda i: (0, 0))
    return pl.pallas_call(
        _ln_kernel, grid=(M // tm,), in_specs=[row, vec, vec], out_specs=(row, row),
        out_shape=(jax.ShapeDtypeStruct((M, D), F32), jax.ShapeDtypeStruct((M, D), BF16)),
        compiler_params=_params(("parallel",)), name="ln",
    )(x, g.reshape(1, D), b.reshape(1, D))


def _halo_specs(T_blocks, cw, col):
    prev = pl.BlockSpec((1, 1, cw), lambda i, j: (jnp.maximum(i - 1, 0), 0, col(j)))
    nxt = pl.BlockSpec((1, 1, cw), lambda i, j: (jnp.minimum(i + 1, T_blocks - 1), 0, col(j)))
    return prev, nxt


def _neighbours(z, last_row, first_row):
    T = z.shape[0]
    i = pl.program_id(0)
    zp = jnp.where(i > 0, last_row, 0.0)
    zn = jnp.where(i < pl.num_programs(0) - 1, first_row, 0.0)
    row = lax.broadcasted_iota(jnp.int32, (T, 1), 0)
    prev = jnp.where(row == 0, zp, pltpu.roll(z, 1, 0))
    nxt = jnp.where(row == T - 1, zn, pltpu.roll(z, T - 1, 0))
    return prev, nxt


def _token_shift(z, last_ref, first_ref, mu):
    prev, nxt = _neighbours(z, last_ref[0], first_ref[0])
    return z + (0.5 * (prev + nxt) - z) * mu


def _split3(w):
    w1 = w.astype(BF16)
    res = w - w1.astype(F32)
    w2 = res.astype(BF16)
    w3 = (res - w2.astype(F32)).astype(BF16)
    return jnp.concatenate([w1, w2, w3], axis=1)


def _seg_sum(x, bd3):
    parts = [jnp.dot(_split3(x[:, i * LANE:(i + 1) * LANE]), bd3, preferred_element_type=F32)
             for i in range(x.shape[1] // LANE)]
    return parts[0] if len(parts) == 1 else jnp.concatenate(parts, axis=1)


def _sigmoid(x):
    return 1.0 / (1.0 + jnp.exp(-x))


def _softplus(x):
    return jnp.maximum(x, 0.0) + jnp.log1p(jnp.exp(-jnp.abs(x)))


def _prep_kernel(pr, pk, pv, hpr, hpk, hpv, hnr, hnk, hnv, plo, hpl, hnl,
                 mur, muk, muv, mul, dbase, wdu, abase, wau, gup, kkw, kaw, rkw, bd,
                 r_o, v_o, kk_o, wl_o, b_o, kd_o, g_o, bg_o, *, gp, lp):
    r = _token_shift(pr[...], hpr, hnr, mur[...])
    k = _token_shift(pk[...], hpk, hnk, muk[...])
    v = _token_shift(pv[...], hpv, hnv, muv[...])
    zl = _token_shift(plo[...], hpl, hnl, mul[...])
    bdm = bd[...]

    g = jnp.dot(_sigmoid(zl[:, :gp]).astype(BF16), gup[...], preferred_element_type=F32)
    kk = k * kkw[...]
    kk = kk / jnp.maximum(jnp.sqrt(_seg_sum(kk * kk, bdm)), 1e-12)
    ka = kaw[...]
    ksum = jnp.zeros_like(k)
    for d in range(2):
        wd = jnp.tanh(zl[:, gp + d * lp:gp + (d + 1) * lp]).astype(BF16)
        x = dbase[d:d + 1, :] + jnp.dot(wd, wdu[d], preferred_element_type=F32)
        wl_o[d] = -jnp.exp(-_softplus(-x) - 0.5)
        ad = zl[:, gp + (2 + d) * lp:gp + (3 + d) * lp].astype(BF16)
        a = _sigmoid(abase[d:d + 1, :] + jnp.dot(ad, wau[d], preferred_element_type=F32))
        kd = k * (1.0 + (a - 1.0) * ka)
        kd_o[d] = kd
        b_o[d] = a * kk
        ksum = ksum + kd
    bonus = _seg_sum(r * ksum * rkw[...], bdm) * v
    r_o[...] = r
    v_o[...] = v
    kk_o[...] = kk
    g_o[...] = g
    bg_o[...] = bonus * g


def _rwkv_prep(prkv, rkv_first, rkv_last, plora, lora_first, lora_last, lw, T, cw):
    S = prkv.shape[0]
    W = prkv.shape[1] // 3
    LW = plora.shape[1]
    gp, lp = lw["gp"], lw["lp"]
    nb, ncb = S // T, W // cw

    def main(off):
        return pl.BlockSpec((T, cw), lambda i, j: (i, j + off * ncb))

    def vec(off, rows=1):
        return pl.BlockSpec((rows, cw), lambda i, j: (0, j + off * ncb))

    hp = [_halo_specs(nb, cw, lambda j, off=off: j + off * ncb) for off in range(3)]
    hpl, hnl = _halo_specs(nb, LW, lambda j: 0)
    in_specs = [main(0), main(1), main(2), hp[0][0], hp[1][0], hp[2][0], hp[0][1], hp[1][1], hp[2][1],
                pl.BlockSpec((T, LW), lambda i, j: (i, 0)), hpl, hnl,
                vec(0), vec(1), vec(2),
                pl.BlockSpec((1, LW), lambda i, j: (0, 0)),
                vec(0, 2),
                pl.BlockSpec((2, lp, cw), lambda i, j: (0, 0, j)),
                vec(0, 2),
                pl.BlockSpec((2, lp, cw), lambda i, j: (0, 0, j)),
                pl.BlockSpec((gp, cw), lambda i, j: (0, j)),
                vec(0), vec(0), vec(0),
                pl.BlockSpec((3 * LANE, LANE), lambda i, j: (0, 0))]
    one = pl.BlockSpec((T, cw), lambda i, j: (i, j))
    two = pl.BlockSpec((2, T, cw), lambda i, j: (0, i, j))
    sd1 = jax.ShapeDtypeStruct((S, W), F32)
    sd2 = jax.ShapeDtypeStruct((2, S, W), F32)
    return pl.pallas_call(
        functools.partial(_prep_kernel, gp=gp, lp=lp),
        grid=(nb, ncb),
        in_specs=in_specs,
        out_specs=(one, one, one, two, two, two, one, one),
        out_shape=(sd1, sd1, sd1, sd2, sd2, sd2, sd1, sd1),
        compiler_params=_params(("parallel", "parallel")),
        name="rwkv_prep",
    )(prkv, prkv, prkv, rkv_last, rkv_last, rkv_last, rkv_first, rkv_first, rkv_first,
      plora, lora_last, lora_first,
      lw["mu_rkv"], lw["mu_rkv"], lw["mu_rkv"], lw["mu_lora"], lw["decay_base"], lw["w_decay_up"],
      lw["aaa_base"], lw["w_aaa_up"], lw["g_up"], lw["k_k"], lw["k_a"], lw["r_k"], lw["bd"])


def _dot_nt(a, b):
    return lax.dot_general(a, b, (((1,), (1,)), ((), ())), preferred_element_type=F32)


def _dot_tn(a, b):
    return lax.dot_general(a, b, (((0,), (0,)), ((), ())), preferred_element_type=F32)


def _dot(a, b):
    return jnp.dot(a, b, preferred_element_type=F32)


def _scan_kernel(r_ref, v_ref, kk_ref, wl_ref, b_ref, kd_ref, y_ref,
                 state, *, C, n_slabs, hs, unroll):
    d = pl.program_id(0)
    c = pl.program_id(1)

    @pl.when(c == 0)
    def _():
        state[...] = jnp.zeros_like(state)

    sgn = 1 - 2 * d
    ri = lax.broadcasted_iota(jnp.int32, (C, C), 0)
    ci = lax.broadcasted_iota(jnp.int32, (C, C), 1)
    tri = ((ri - ci) * sgn >= 0).astype(BF16)

    n_heads = LANE // hs
    R = n_heads * C
    lc = int(math.log2(C))
    lh = int(math.log2(hs))
    n_sq = lc
    ri = lax.broadcasted_iota(jnp.int32, (C, 2 * R), 0)
    ci = lax.broadcasted_iota(jnp.int32, (C, 2 * R), 1)
    diff = (ri - (ci & (C - 1))) * sgn
    strict = diff > 0
    incl = diff >= 0
    sr = lax.broadcasted_iota(jnp.int32, (LANE, LANE), 0)
    sc = lax.broadcasted_iota(jnp.int32, (LANE, LANE), 1)
    same_head = (sr >> lh) == (sc >> lh)

    def stack(x, shift):
        grp = lax.broadcasted_iota(jnp.int32, x.shape, 1) >> shift
        return jnp.concatenate(
            [jnp.where(grp == h, x, jnp.zeros_like(x)) for h in range(n_heads)], axis=0)

    def lanes(p):
        return pl.ds(pl.multiple_of(p * LANE, LANE), LANE)

    def body(q, carry):
        ps = [q * unroll + j for j in range(unroll)]
        n = range(unroll)
        wl = [wl_ref[0, :, lanes(p)] for p in ps]
        s0 = [state[p] for p in ps]
        v = [v_ref[:, lanes(p)].astype(BF16) for p in ps]
        cum3 = [_dot(tri, _split3(wl[j])) for j in n]
        cum = [cum3[j][:, :LANE] + cum3[j][:, LANE:2 * LANE] + cum3[j][:, 2 * LANE:] for j in n]
        tot = [jnp.sum(wl[j], axis=0, keepdims=True) for j in n]
        m = [0.5 * tot[j] for j in n]
        e_iv = [jnp.exp(m[j] - cum[j]) for j in n]
        at = [(-kk_ref[:, lanes(ps[j])] * jnp.exp(cum[j] - wl[j] - m[j])).astype(BF16) for j in n]
        rt = [(r_ref[:, lanes(ps[j])] * jnp.exp(cum[j] - m[j])).astype(BF16) for j in n]
        bt = [(b_ref[0, :, lanes(ps[j])] * e_iv[j]).astype(BF16) for j in n]
        kt = [(kd_ref[0, :, lanes(ps[j])] * e_iv[j]).astype(BF16) for j in n]
        em = [jnp.exp(m[j]) for j in n]
        et = [jnp.exp(tot[j]) for j in n]
        s0s = [(s0[j] * em[j]).astype(BF16) for j in n]
        ar = [jnp.concatenate([at[j], rt[j]], axis=0) for j in n]
        big = [_dot_nt(ar[j], jnp.concatenate([stack(bt[j], lh), stack(kt[j], lh)], axis=0))
               for j in n]
        sx = [_dot_nt(ar[j], s0s[j]) for j in n]
        top = [jnp.where(strict, big[j][:C], jnp.zeros((C, 2 * R), F32)) for j in n]
        lab = [top[j][:, :R].astype(BF16) for j in n]
        lak = [top[j][:, R:].astype(BF16) for j in n]
        arbk = [jnp.where(incl, big[j][C:], jnp.zeros((C, 2 * R), F32)).astype(BF16) for j in n]
        v_s = [stack(v[j], lh) for j in n]
        x = [sx[j][:C] + _dot(lak[j], v_s[j]) for j in n]
        for i in range(n_sq - 1):
            both = [_dot(lab[j], jnp.concatenate([stack(x[j].astype(BF16), lh), stack(lab[j], lc)], axis=1))
                    for j in n]
            x = [x[j] + both[j][:, :LANE] for j in n]
            lab = [both[j][:, LANE:].astype(BF16) for j in n]
        u = [(x[j] + _dot(lab[j], stack(x[j].astype(BF16), lh))).astype(BF16) for j in n]
        y = [sx[j][C:] + _dot(arbk[j], jnp.concatenate([stack(u[j], lh), v_s[j]], axis=0)) for j in n]
        upd = [_dot_tn(jnp.concatenate([u[j], v[j]], axis=0), jnp.concatenate([bt[j], kt[j]], axis=0))
               for j in n]
        for j in n:
            y_ref[0, :, lanes(ps[j])] = y[j]
            state[ps[j]] = s0[j] * et[j] + jnp.where(same_head, upd[j], jnp.zeros_like(upd[j])) * em[j]
        return carry

    lax.fori_loop(0, n_slabs // unroll, body, 0)


def _rwkv_scan(r, v, kk, wl, b, kd):
    S, W = r.shape
    C = CHUNK
    nc = S // C
    n_slabs = W // LANE

    def shared(d, c):
        return (c + d * (nc - 1 - 2 * c), 0)

    def perdir(d, c):
        return (d, c + d * (nc - 1 - 2 * c), 0)

    s1 = pl.BlockSpec((C, W), shared)
    s2 = pl.BlockSpec((1, C, W), perdir)
    return pl.pallas_call(
        functools.partial(_scan_kernel, C=C, n_slabs=n_slabs, hs=HEAD_SIZE,
                          unroll=math.gcd(n_slabs, SCAN_UNROLL)),
        grid=(2, nc),
        in_specs=[s1, s1, s1, s2, s2, s2],
        out_specs=s2,
        out_shape=jax.ShapeDtypeStruct((2, S, W), F32),
        scratch_shapes=[pltpu.VMEM((n_slabs, LANE, LANE), F32)],
        compiler_params=_params(("arbitrary", "arbitrary")),
        name="rwkv_scan",
    )(r, v, kk, wl, b, kd)


def _fin_kernel(y_ref, g_ref, bg_ref, lg_ref, lb_ref, bd_ref, o_ref, *, hs):
    y = y_ref[0] + y_ref[1]
    bd = bd_ref[...]
    mean = _seg_sum(y, bd) * (1.0 / hs)
    yc = y - mean
    var = _seg_sum(yc * yc, bd) * (1.0 / hs)
    yn = yc * lax.rsqrt(var + GN_EPS) * lg_ref[...] + lb_ref[...]
    o_ref[...] = (yn * g_ref[...] + bg_ref[...]).astype(o_ref.dtype)


def _rwkv_finish(y2, g, bg, lw, T, cw):
    _, S, W = y2.shape
    one = pl.BlockSpec((T, cw), lambda i, j: (i, j))
    vec = pl.BlockSpec((1, cw), lambda i, j: (0, j))
    return pl.pallas_call(
        functools.partial(_fin_kernel, hs=HEAD_SIZE),
        grid=(S // T, W // cw),
        in_specs=[pl.BlockSpec((2, T, cw), lambda i, j: (0, i, j)), one, one, vec, vec,
                  pl.BlockSpec((3 * LANE, LANE), lambda i, j: (0, 0))],
        out_specs=one,
        out_shape=jax.ShapeDtypeStruct((S, W), BF16),
        compiler_params=_params(("parallel", "parallel")),
        name="rwkv_finish",
    )(y2, g, bg, lw["lnx_g"], lw["lnx_b"], lw["bd"])


def _dft1_kernel(x_ref, cs_ref, tc_ref, ts_ref, o_ref, *, n1):
    g = _dot(cs_ref[...], x_ref[...])
    gr, gi = g[:n1], g[n1:]
    reps = x_ref.shape[1] // LANE
    tc = jnp.concatenate([tc_ref[0]] * reps, axis=1)
    ts = jnp.concatenate([ts_ref[0]] * reps, axis=1)
    o_ref[0] = (gr * tc + gi * ts).astype(o_ref.dtype)
    o_ref[1] = (gi * tc - gr * ts).astype(o_ref.dtype)


def _dft2_kernel(g_ref, m2_ref, cc_ref, sc_ref, o_ref, *, n2, gw):
    fw = g_ref.shape[-1]
    g = g_ref[...].reshape(2 * n2, fw)
    ab = _dot(m2_ref[...], g)
    a = ab[:n2].astype(BF16)
    b = ab[n2:].astype(BF16)
    cc = cc_ref[...]
    sc = sc_ref[...]
    for q in range(fw // gw):
        sl = slice(q * gw, (q + 1) * gw)
        o_ref[:, sl] = (_dot(a[:, sl], cc) + _dot(b[:, sl], sc)).astype(o_ref.dtype)


def _fourier_tables(S, gw):
    n1 = 1 << (int(math.log2(S)) // 2)
    n2 = S // n1
    assert n1 * n2 == S
    i1 = np.arange(n1)
    ang1 = 2.0 * np.pi * np.outer(i1, i1) / n1
    cs1 = np.concatenate([np.cos(ang1), -np.sin(ang1)], axis=0)
    i2 = np.arange(n2)
    angt = 2.0 * np.pi * np.outer(i2, i1) / S
    ang2 = 2.0 * np.pi * np.outer(i2, i2) / n2
    c2, s2 = np.cos(ang2), np.sin(ang2)
    m2 = np.block([[c2, s2], [-s2, c2]])
    ic = np.arange(gw)
    angc = 2.0 * np.pi * np.outer(ic, ic) / gw
    scale = 1.0 / math.sqrt(S * gw)
    return dict(n1=n1, n2=n2,
                cs1=jnp.asarray(cs1, BF16),
                tc=jnp.asarray(np.cos(angt), F32), ts=jnp.asarray(np.sin(angt), F32),
                m2=jnp.asarray(m2, BF16),
                cc=jnp.asarray(np.cos(angc) * scale, BF16), sc=jnp.asarray(np.sin(angc) * scale, BF16))


def _fourier_mixer(pf, tab):
    S, Fw = pf.shape
    n1, n2 = tab["n1"], tab["n2"]
    gw = Fw // N_FOURIER_GROUPS
    tc = jnp.broadcast_to(tab["tc"][:, :, None], (n2, n1, LANE))
    ts = jnp.broadcast_to(tab["ts"][:, :, None], (n2, n1, LANE))
    x2 = pf.reshape(n1, n2 * Fw)
    g = pl.pallas_call(
        functools.partial(_dft1_kernel, n1=n1),
        grid=(n2,),
        in_specs=[pl.BlockSpec((n1, Fw), lambda j: (0, j)),
                  pl.BlockSpec((2 * n1, n1), lambda j: (0, 0)),
                  pl.BlockSpec((1, n1, LANE), lambda j: (j, 0, 0)),
                  pl.BlockSpec((1, n1, LANE), lambda j: (j, 0, 0))],
        out_specs=pl.BlockSpec((2, n1, Fw), lambda j: (0, 0, j)),
        out_shape=jax.ShapeDtypeStruct((2, n1, n2 * Fw), BF16),
        compiler_params=_params(("parallel",)),
        name="dft_stage1",
    )(x2, tab["cs1"], tc, ts)
    g4 = g.reshape(2, n1, n2, Fw)
    out = pl.pallas_call(
        functools.partial(_dft2_kernel, n2=n2, gw=gw),
        grid=(n1,),
        in_specs=[pl.BlockSpec((2, None, n2, Fw), lambda i: (0, i, 0, 0)),
                  pl.BlockSpec((2 * n2, 2 * n2), lambda i: (0, 0)),
                  pl.BlockSpec((gw, gw), lambda i: (0, 0)),
                  pl.BlockSpec((gw, gw), lambda i: (0, 0))],
        out_specs=pl.BlockSpec((n2, Fw), lambda i: (0, i)),
        out_shape=jax.ShapeDtypeStruct((n2, n1 * Fw), BF16),
        compiler_params=_params(("parallel",)),
        name="dft_stage2",
    )(g4, tab["m2"], tab["cc"], tab["sc"])
    return out.reshape(S, Fw)


def _acc_dot(o_ref, x, w_ref):
    D = o_ref.shape[1]
    pn = _pick(D, 512)
    for n in range(0, D, pn):
        o_ref[:, n:n + pn] += _dot(x, w_ref[:, n:n + pn])


def _res_ln_finish(k, h_ref, g_ref, b_ref, o_ref, ob_ref, alpha):
    @pl.when(k == pl.num_programs(1) - 1)
    def _():
        T = o_ref.shape[0]
        for r in range(0, T, 16):
            for r8 in (r, r + 8):
                rows = pl.ds(r8, 8)
                o_ref[rows, :] = _ln_rows(alpha * h_ref[rows, :] + o_ref[rows, :], g_ref[...], b_ref[...])
            ob_ref[pl.ds(r, 16), :] = o_ref[pl.ds(r, 16), :].astype(BF16)


def _proj_ln_kernel(x_ref, w_ref, h_ref, g_ref, b_ref, o_ref, ob_ref, *, alpha, tn):
    n = pl.program_id(1)
    cols = pl.ds(pl.multiple_of(n * tn, LANE), tn)
    o_ref[:, cols] = _dot(x_ref[...], w_ref[...])
    _res_ln_finish(n, h_ref, g_ref, b_ref, o_ref, ob_ref, alpha)


def _ffn_tail_kernel(ug, uv, lg, lv, fg, fv, cg, cv, bg, bv, w_ref, h_ref, g_ref, b_ref, o_ref, ob_ref,
                     act_a, act_b, *, alpha):
    k = pl.program_id(1)

    @pl.when(k == 0)
    def _():
        o_ref[...] = jnp.zeros_like(o_ref)
        act_a[...] = jnp.zeros_like(act_a)

    T, D = o_ref.shape
    tk = act_a.shape[1]
    pn = _pick(D, 512)
    n_pan = D // pn
    rb = _pick(T, 64, 16)
    i = pl.program_id(0)
    first_tile = i == 0
    last_tile = i == pl.num_programs(0) - 1
    row = lax.broadcasted_iota(jnp.int32, (rb, 1), 0)

    def conv(u_ref, last_ref, first_ref, c_ref, cb_ref, r0, cols):
        u = u_ref[r0:r0 + rb, cols]
        if r0 == 0:
            up = jnp.where(first_tile, 0.0, last_ref[0, :, cols])
        else:
            up = u_ref[r0 - 1:r0, cols]
        if r0 + rb == T:
            un = jnp.where(last_tile, 0.0, first_ref[0, :, cols])
        else:
            un = u_ref[r0 + rb:r0 + rb + 1, cols]
        prev = jnp.where(row == 0, up, pltpu.roll(u, 1, 0))
        nxt = jnp.where(row == rb - 1, un, pltpu.roll(u, rb - 1, 0))
        return (c_ref[0:1, cols] * prev + c_ref[1:2, cols] * u + c_ref[2:3, cols] * nxt
                + cb_ref[:, cols])

    blocks = [(r0, slice(c, c + LANE)) for c in range(0, tk, LANE) for r0 in range(0, T, rb)]

    def step(act_in, act_out):
        for q in range(n_pan):
            n = q * pn
            o_ref[:, n:n + pn] += _dot(act_in[...], w_ref[:, n:n + pn])
            for r0, cols in blocks[q * len(blocks) // n_pan:(q + 1) * len(blocks) // n_pan]:
                gate = conv(ug, lg, fg, cg, bg, r0, cols)
                val = conv(uv, lv, fv, cv, bv, r0, cols)
                gelu = 0.5 * gate * (1.0 + lax.erf(gate * (1.0 / math.sqrt(2.0))))
                act_out[r0:r0 + rb, cols] = (gelu * val).astype(BF16)

    @pl.when(k % 2 == 0)
    def _():
        step(act_a, act_b)

    @pl.when(k % 2 == 1)
    def _():
        step(act_b, act_a)

    _res_ln_finish(k, h_ref, g_ref, b_ref, o_ref, ob_ref, alpha)


def _row_tile_specs(T, D):
    h_spec = pl.BlockSpec((T, D), lambda i, k: (i, 0), pipeline_mode=pl.Buffered(1))
    vec = pl.BlockSpec((8, D), lambda i, k: (0, 0))
    out = pl.BlockSpec((T, D), lambda i, k: (i, 0))
    return h_spec, vec, out


def _proj_ln(x, w, h, g, b, alpha, T=512, tn=512, name="proj_ln"):
    M, K = x.shape
    D = w.shape[1]
    T = _pick(M, T, 8)
    tn = _pick(D, tn)
    h_spec, vec, out = _row_tile_specs(T, D)
    return pl.pallas_call(
        functools.partial(_proj_ln_kernel, alpha=alpha, tn=tn),
        grid=(M // T, D // tn),
        in_specs=[pl.BlockSpec((T, K), lambda i, n: (i, 0), pipeline_mode=pl.Buffered(1)),
                  pl.BlockSpec((K, tn), lambda i, n: (0, n)), h_spec, vec, vec],
        out_specs=(out, out),
        out_shape=(jax.ShapeDtypeStruct((M, D), F32), jax.ShapeDtypeStruct((M, D), BF16)),
        compiler_params=_params(("parallel", "arbitrary")),
        name=name,
    )(x, w, h, jnp.broadcast_to(g, (8, D)), jnp.broadcast_to(b, (8, D)))


def _ffn_tail(u, u_first, u_last, conv_w, conv_b, w_down, h, g, b, alpha, T, tk=512):
    S, N2 = u.shape
    dffp = N2 // 2
    D = w_down.shape[1]
    tk = _pick(dffp, tk)
    nk = dffp // tk
    nb = S // T
    h_spec, vec, out = _row_tile_specs(T, D)

    def made(k):
        return jnp.minimum(k, nk - 1)

    def main(off):
        return pl.BlockSpec((T, tk), lambda i, k: (i, made(k) + off * nk))

    def cvec(off, rows):
        return pl.BlockSpec((rows, tk), lambda i, k: (0, made(k) + off * nk))

    hg = _halo_specs(nb, tk, made)
    hv = _halo_specs(nb, tk, lambda k: made(k) + nk)
    return pl.pallas_call(
        functools.partial(_ffn_tail_kernel, alpha=alpha),
        grid=(nb, nk + 1),
        in_specs=[main(0), main(1), hg[0], hv[0], hg[1], hv[1],
                  cvec(0, 3), cvec(1, 3), cvec(0, 1), cvec(1, 1),
                  pl.BlockSpec((tk, D), lambda i, k: (jnp.maximum(k - 1, 0), 0)), h_spec, vec, vec],
        out_specs=(out, out),
        out_shape=(jax.ShapeDtypeStruct((S, D), F32), jax.ShapeDtypeStruct((S, D), BF16)),
        scratch_shapes=[pltpu.VMEM((T, tk), BF16), pltpu.VMEM((T, tk), BF16)],
        compiler_params=_params(("parallel", "arbitrary")),
        name="ffn_tail",
    )(u, u, u_last, u_last, u_first, u_first, conv_w, conv_w, conv_b, conv_b, w_down, h,
      jnp.broadcast_to(g, (8, D)), jnp.broadcast_to(b, (8, D)))


def _layer_weights(l, W, gl, dl, al, w_in, mu_shift, decay_base, w_decay_up, aaa_base, w_aaa_up,
                   g_lora_up, k_k, k_a, r_k, lnx_g, lnx_b):
    gp = _round_up(gl, LANE)
    lp = _round_up(max(dl, al), LANE)
    o = 3 * W

    def padc(a, n):
        return jnp.pad(a, ((0, 0), (0, n - a.shape[1])))

    def padr(a, n):
        return jnp.pad(a, ((0, 0), (0, n - a.shape[1]), (0, 0)))

    wl, ml = w_in[l], mu_shift[l][None, :]
    segs = [(o, gl, gp), (o + gl, dl, lp), (o + gl + dl, dl, lp),
            (o + gl + 2 * dl, al, lp), (o + gl + 2 * dl + al, al, lp)]
    w_lora = jnp.concatenate([padc(wl[:, s:s + n], p) for s, n, p in segs], axis=1)
    mu_lora = jnp.concatenate([padc(ml[:, s:s + n], p) for s, n, p in segs], axis=1)
    n_shift = o + gl + 2 * dl + 2 * al
    hs = HEAD_SIZE
    idx = np.arange(LANE)
    bd = jnp.asarray(np.tile((idx[:, None] // hs) == (idx[None, :] // hs), (3, 1)), BF16)
    return dict(
        gp=gp, lp=lp,
        w_rkv=wl[:, :o].astype(BF16), w_lora=w_lora.astype(BF16), w_f=wl[:, n_shift:].astype(BF16),
        mu_rkv=ml[:, :o], mu_lora=mu_lora,
        decay_base=decay_base[l], aaa_base=aaa_base[l],
        w_decay_up=padr(w_decay_up[l], lp).astype(BF16), w_aaa_up=padr(w_aaa_up[l], lp).astype(BF16),
        g_up=jnp.pad(g_lora_up[l], ((0, gp - gl), (0, 0))).astype(BF16),
        k_k=k_k[l][None, :], k_a=k_a[l][None, :], r_k=r_k[l].reshape(1, W),
        lnx_g=lnx_g[l][None, :], lnx_b=lnx_b[l][None, :], bd=bd)


def kernel(x, ln0_g, ln0_b, w_in, mu_shift, decay_base, w_decay_up, aaa_base, w_aaa_up, g_lora_up, k_k, k_a, r_k, lnx_g, lnx_b, w_out, ln1_g, ln1_b, w_ffn_up, ffn_conv_w, ffn_conv_b, w_ffn_down, ln2_g, ln2_b):
    B, S, D = x.shape
    depth = w_in.shape[0]
    W = k_k.shape[1]
    gl, dl, al = g_lora_up.shape[1], w_decay_up.shape[2], w_aaa_up.shape[2]
    n_shift = 3 * W + gl + 2 * dl + 2 * al
    Fw = w_in.shape[2] - n_shift
    dff = w_ffn_down.shape[1]
    alpha = float((2 * depth) ** 0.25)
    assert B == 1 and S % CHUNK == 0 and W % LANE == 0 and LANE % HEAD_SIZE == 0

    tab = _fourier_tables(S, Fw // N_FOURIER_GROUPS)
    Tp = _pick(S, 256, 8)
    cwp = _pick(W, 512)
    Tf = _pick(S, 512, 8)
    dffp = _round_up(dff, 512)
    pad = dffp - dff

    def pad_gv(a):
        cfg = [(0, 0)] * (a.ndim - 1) + [(0, pad)]
        return jnp.concatenate([jnp.pad(a[..., :dff], cfg), jnp.pad(a[..., dff:], cfg)], axis=-1)

    h, hb = _layer_norm(x.reshape(S, D), ln0_g, ln0_b)
    for l in range(depth):
        lw = _layer_weights(l, W, gl, dl, al, w_in, mu_shift, decay_base, w_decay_up, aaa_base,
                            w_aaa_up, g_lora_up, k_k, k_a, r_k, lnx_g, lnx_b)
        prkv, rkv_first, rkv_last = _matmul(hb, lw["w_rkv"], F32, name="in_rkv", edge_rows=Tp)
        plora, lora_first, lora_last = _matmul(hb, lw["w_lora"], F32, name="in_lora", edge_rows=Tp)
        pf = _matmul(hb, lw["w_f"], BF16, name="in_fourier")
        r, v, kk, wl, b, kd, g, bg = _rwkv_prep(prkv, rkv_first, rkv_last, plora, lora_first, lora_last,
                                                lw, Tp, cwp)
        y2 = _rwkv_scan(r, v, kk, wl, b, kd)
        y_rwkv = _rwkv_finish(y2, g, bg, lw, Tp, cwp)
        y_four = _fourier_mixer(pf, tab)
        mix = jnp.concatenate([y_rwkv, y_four], axis=1)
        h, hb = _proj_ln(mix, w_out[l].astype(BF16), h, ln1_g[l], ln1_b[l], alpha, T=Tf, name="out_proj_ln")
        u, u_first, u_last = _matmul(hb, pad_gv(w_ffn_up[l]).astype(BF16), F32, name="ffn_up", edge_rows=Tf)
        w_down = jnp.pad(w_ffn_down[l], ((0, pad), (0, 0))).astype(BF16)
        h, hb = _ffn_tail(u, u_first, u_last, pad_gv(ffn_conv_w[l]), pad_gv(ffn_conv_b[l][None, :]),
                          w_down, h, ln2_g[l], ln2_b[l], alpha, Tf)
    return h.reshape(B, S, D)
```

```python
import functools
import math

import numpy as np
import jax
import jax.numpy as jnp
from jax import lax
from jax.experimental import pallas as pl
from jax.experimental.pallas import tpu as pltpu

F32 = jnp.float32
BF16 = jnp.bfloat16

LANE = 128
HEAD_SIZE = 64
N_FOURIER_GROUPS = 8
LN_EPS = 1e-5
GN_EPS = 64e-5
CHUNK = 64
SCAN_UNROLL = 16
VMEM_LIMIT = 56 * 1024 * 1024


def _round_up(n, m):
    return (n + m - 1) // m * m


def _pick(dim, target, align=LANE):
    if dim <= target:
        return dim
    best = None
    for t in range(align, target + 1, align):
        if dim % t == 0:
            best = t
    assert best is not None, (dim, target, align)
    return best


def _params(sem):
    return pltpu.CompilerParams(dimension_semantics=sem, vmem_limit_bytes=VMEM_LIMIT)


def _mm_kernel(x_ref, w_ref, o_ref):
    o_ref[...] = jnp.dot(x_ref[...], w_ref[...], preferred_element_type=F32).astype(o_ref.dtype)


def _mm_acc_kernel(x_ref, w_ref, o_ref, acc_ref):
    k = pl.program_id(2)

    @pl.when(k == 0)
    def _():
        acc_ref[...] = jnp.zeros_like(acc_ref)

    acc_ref[...] += jnp.dot(x_ref[...], w_ref[...], preferred_element_type=F32)

    @pl.when(k == pl.num_programs(2) - 1)
    def _():
        o_ref[...] = acc_ref[...].astype(o_ref.dtype)


def _mm_edges_kernel(x_ref, w_ref, o_ref, f_ref, l_ref, *, tb):
    res = jnp.dot(x_ref[...], w_ref[...], preferred_element_type=F32)
    o_ref[...] = res.astype(o_ref.dtype)
    for q in range(res.shape[0] // tb):
        f_ref[q] = res[q * tb:q * tb + 1, :]
        l_ref[q] = res[(q + 1) * tb - 1:(q + 1) * tb, :]


def _matmul(x, w, out_dtype, tm=1024, tn=1024, tk=4096, name="mm", edge_rows=None):
    M, K = x.shape
    N = w.shape[1]
    tm = _pick(M, tm, 8)
    tn = _pick(N, tn)
    tk = _pick(K, tk)
    if edge_rows is not None:
        assert tk == K and tm % edge_rows == 0
        nq = tm // edge_rows
        edge = pl.BlockSpec((nq, 1, tn), lambda j, i: (i, 0, j))
        esd = jax.ShapeDtypeStruct((M // edge_rows, 1, N), F32)
        return pl.pallas_call(
            functools.partial(_mm_edges_kernel, tb=edge_rows),
            grid=(N // tn, M // tm),
            in_specs=[pl.BlockSpec((tm, K), lambda j, i: (i, 0)),
                      pl.BlockSpec((K, tn), lambda j, i: (0, j))],
            out_specs=(pl.BlockSpec((tm, tn), lambda j, i: (i, j)), edge, edge),
            out_shape=(jax.ShapeDtypeStruct((M, N), out_dtype), esd, esd),
            compiler_params=_params(("parallel", "parallel")),
            name=name,
        )(x, w)
    if tk == K:
        return pl.pallas_call(
            _mm_kernel,
            grid=(N // tn, M // tm),
            in_specs=[pl.BlockSpec((tm, K), lambda j, i: (i, 0)),
                      pl.BlockSpec((K, tn), lambda j, i: (0, j))],
            out_specs=pl.BlockSpec((tm, tn), lambda j, i: (i, j)),
            out_shape=jax.ShapeDtypeStruct((M, N), out_dtype),
            compiler_params=_params(("parallel", "parallel")),
            name=name,
        )(x, w)
    return pl.pallas_call(
        _mm_acc_kernel,
        grid=(N // tn, M // tm, K // tk),
        in_specs=[pl.BlockSpec((tm, tk), lambda j, i, k: (i, k)),
                  pl.BlockSpec((tk, tn), lambda j, i, k: (k, j))],
        out_specs=pl.BlockSpec((tm, tn), lambda j, i, k: (i, j)),
        out_shape=jax.ShapeDtypeStruct((M, N), out_dtype),
        scratch_shapes=[pltpu.VMEM((tm, tn), F32)],
        compiler_params=_params(("parallel", "parallel", "arbitrary")),
        name=name,
    )(x, w)


def _ln_rows(x, g, b):
    mu = jnp.mean(x, axis=-1, keepdims=True)
    xc = x - mu
    var = jnp.mean(xc * xc, axis=-1, keepdims=True)
    return xc * lax.rsqrt(var + LN_EPS) * g + b


def _ln_kernel(x_ref, g_ref, b_ref, o_ref, ob_ref):
    y = _ln_rows(x_ref[...], g_ref[...], b_ref[...])
    o_ref[...] = y
    ob_ref[...] = y.astype(BF16)


def _layer_norm(x, g, b):
    M, D = x.shape
    tm = _pick(M, 256, 8)
    row = pl.BlockSpec((tm, D), lambda i: (i, 0))
    vec = pl.BlockSpec((1, D), lambda i: (0, 0))
    return pl.pallas_call(
        _ln_kernel, grid=(M // tm,), in_specs=[row, vec, vec], out_specs=(row, row),
        out_shape=(jax.ShapeDtypeStruct((M, D), F32), jax.ShapeDtypeStruct((M, D), BF16)),
        compiler_params=_params(("parallel",)), name="ln",
    )(x, g.reshape(1, D), b.reshape(1, D))


def _halo_specs(T_blocks, cw, col):
    prev = pl.BlockSpec((1, 1, cw), lambda i, j: (jnp.maximum(i - 1, 0), 0, col(j)))
    nxt = pl.BlockSpec((1, 1, cw), lambda i, j: (jnp.minimum(i + 1, T_blocks - 1), 0, col(j)))
    return prev, nxt


def _neighbours(z, last_row, first_row):
    T = z.shape[0]
    i = pl.program_id(0)
    zp = jnp.where(i > 0, last_row, 0.0)
    zn = jnp.where(i < pl.num_programs(0) - 1, first_row, 0.0)
    row = lax.broadcasted_iota(jnp.int32, (T, 1), 0)
    prev = jnp.where(row == 0, zp, pltpu.roll(z, 1, 0))
    nxt = jnp.where(row == T - 1, zn, pltpu.roll(z, T - 1, 0))
    return prev, nxt


def _token_shift(z, last_ref, first_ref, mu):
    prev, nxt = _neighbours(z, last_ref[0], first_ref[0])
    return z + (0.5 * (prev + nxt) - z) * mu


def _split3(w):
    w1 = w.astype(BF16)
    res = w - w1.astype(F32)
    w2 = res.astype(BF16)
    w3 = (res - w2.astype(F32)).astype(BF16)
    return jnp.concatenate([w1, w2, w3], axis=1)


def _seg_sum(x, bd3):
    parts = [jnp.dot(_split3(x[:, i * LANE:(i + 1) * LANE]), bd3, preferred_element_type=F32)
             for i in range(x.shape[1] // LANE)]
    return parts[0] if len(parts) == 1 else jnp.concatenate(parts, axis=1)


def _sigmoid(x):
    return 1.0 / (1.0 + jnp.exp(-x))


def _prep_kernel(pr, pk, pv, hpr, hpk, hpv, hnr, hnk, hnv, plo, hpl, hnl,
                 mur, muk, muv, mul, dbase, wdu, abase, wau, gup, kkw, kaw, rkw, bd,
                 r_o, v_o, kk_o, wl_o, b_o, kd_o, g_o, bg_o, *, gp, lp):
    r = _token_shift(pr[...], hpr, hnr, mur[...])
    k = _token_shift(pk[...], hpk, hnk, muk[...])
    v = _token_shift(pv[...], hpv, hnv, muv[...])
    zl = _token_shift(plo[...], hpl, hnl, mul[...])
    bdm = bd[...]

    g = jnp.dot(_sigmoid(zl[:, :gp]).astype(BF16), gup[...], preferred_element_type=F32)
    kk = k * kkw[...]
    kk = kk / jnp.maximum(jnp.sqrt(_seg_sum(kk * kk, bdm)), 1e-12)
    ka = kaw[...]
    ksum = jnp.zeros_like(k)
    for d in range(2):
        wd = jnp.tanh(zl[:, gp + d * lp:gp + (d + 1) * lp]).astype(BF16)
        x = dbase[d:d + 1, :] + jnp.dot(wd, wdu[d], preferred_element_type=F32)
        wl_o[d] = -math.exp(-0.5) * _sigmoid(x)
        ad = zl[:, gp + (2 + d) * lp:gp + (3 + d) * lp].astype(BF16)
        a = _sigmoid(abase[d:d + 1, :] + jnp.dot(ad, wau[d], preferred_element_type=F32))
        kd = k * (1.0 + (a - 1.0) * ka)
        kd_o[d] = kd
        b_o[d] = a * kk
        ksum = ksum + kd
    bonus = _seg_sum(r * ksum * rkw[...], bdm) * v
    r_o[...] = r
    v_o[...] = v
    kk_o[...] = kk
    g_o[...] = g
    bg_o[...] = bonus * g


def _rwkv_prep(prkv, rkv_first, rkv_last, plora, lora_first, lora_last, lw, T, cw):
    S = prkv.shape[0]
    W = prkv.shape[1] // 3
    LW = plora.shape[1]
    gp, lp = lw["gp"], lw["lp"]
    nb, ncb = S // T, W // cw

    def main(off):
        return pl.BlockSpec((T, cw), lambda i, j: (i, j + off * ncb))

    def vec(off, rows=1):
        return pl.BlockSpec((rows, cw), lambda i, j: (0, j + off * ncb))

    hp = [_halo_specs(nb, cw, lambda j, off=off: j + off * ncb) for off in range(3)]
    hpl, hnl = _halo_specs(nb, LW, lambda j: 0)
    in_specs = [main(0), main(1), main(2), hp[0][0], hp[1][0], hp[2][0], hp[0][1], hp[1][1], hp[2][1],
                pl.BlockSpec((T, LW), lambda i, j: (i, 0)), hpl, hnl,
                vec(0), vec(1), vec(2),
                pl.BlockSpec((1, LW), lambda i, j: (0, 0)),
                vec(0, 2),
                pl.BlockSpec((2, lp, cw), lambda i, j: (0, 0, j)),
                vec(0, 2),
                pl.BlockSpec((2, lp, cw), lambda i, j: (0, 0, j)),
                pl.BlockSpec((gp, cw), lambda i, j: (0, j)),
                vec(0), vec(0), vec(0),
                pl.BlockSpec((3 * LANE, LANE), lambda i, j: (0, 0))]
    one = pl.BlockSpec((T, cw), lambda i, j: (i, j))
    two = pl.BlockSpec((2, T, cw), lambda i, j: (0, i, j))
    sd1 = jax.ShapeDtypeStruct((S, W), F32)
    sd2 = jax.ShapeDtypeStruct((2, S, W), F32)
    return pl.pallas_call(
        functools.partial(_prep_kernel, gp=gp, lp=lp),
        grid=(nb, ncb),
        in_specs=in_specs,
        out_specs=(one, one, one, two, two, two, one, one),
        out_shape=(sd1, sd1, sd1, sd2, sd2, sd2, sd1, sd1),
        compiler_params=_params(("parallel", "parallel")),
        name="rwkv_prep",
    )(prkv, prkv, prkv, rkv_last, rkv_last, rkv_last, rkv_first, rkv_first, rkv_first,
      plora, lora_last, lora_first,
      lw["mu_rkv"], lw["mu_rkv"], lw["mu_rkv"], lw["mu_lora"], lw["decay_base"], lw["w_decay_up"],
      lw["aaa_base"], lw["w_aaa_up"], lw["g_up"], lw["k_k"], lw["k_a"], lw["r_k"], lw["bd"])


def _dot_nt(a, b):
    return lax.dot_general(a, b, (((1,), (1,)), ((), ())), preferred_element_type=F32)


def _dot_tn(a, b):
    return lax.dot_general(a, b, (((0,), (0,)), ((), ())), preferred_element_type=F32)


def _dot(a, b):
    return jnp.dot(a, b, preferred_element_type=F32)


def _scan_kernel(r_ref, v_ref, kk_ref, wl_ref, b_ref, kd_ref, y_ref,
                 state, *, C, n_slabs, hs, unroll):
    d = pl.program_id(0)
    c = pl.program_id(1)

    @pl.when(c == 0)
    def _():
        state[...] = jnp.zeros_like(state)

    sgn = 1 - 2 * d
    ri = lax.broadcasted_iota(jnp.int32, (C, C), 0)
    ci = lax.broadcasted_iota(jnp.int32, (C, C), 1)
    tri = ((ri - ci) * sgn >= 0).astype(BF16)

    n_heads = LANE // hs
    R = n_heads * C
    lc = int(math.log2(C))
    lh = int(math.log2(hs))
    n_sq = lc
    ri = lax.broadcasted_iota(jnp.int32, (C, 2 * R), 0)
    ci = lax.broadcasted_iota(jnp.int32, (C, 2 * R), 1)
    diff = (ri - (ci & (C - 1))) * sgn
    strict = diff > 0
    incl = diff >= 0
    sr = lax.broadcasted_iota(jnp.int32, (LANE, LANE), 0)
    sc = lax.broadcasted_iota(jnp.int32, (LANE, LANE), 1)
    same_head = (sr >> lh) == (sc >> lh)

    def stack(x, shift):
        grp = lax.broadcasted_iota(jnp.int32, x.shape, 1) >> shift
        return jnp.concatenate(
            [jnp.where(grp == h, x, jnp.zeros_like(x)) for h in range(n_heads)], axis=0)

    def lanes(p):
        return pl.ds(pl.multiple_of(p * LANE, LANE), LANE)

    def body(q, carry):
        ps = [q * unroll + j for j in range(unroll)]
        n = range(unroll)
        wl = [wl_ref[0, :, lanes(p)] for p in ps]
        s0 = [state[p] for p in ps]
        v = [v_ref[:, lanes(p)].astype(BF16) for p in ps]
        cum3 = [_dot(tri, _split3(wl[j])) for j in n]
        cum = [cum3[j][:, :LANE] + cum3[j][:, LANE:2 * LANE] + cum3[j][:, 2 * LANE:] for j in n]
        tot = [jnp.sum(wl[j], axis=0, keepdims=True) for j in n]
        m = [0.5 * tot[j] for j in n]
        e_iv = [jnp.exp(m[j] - cum[j]) for j in n]
        at = [(-kk_ref[:, lanes(ps[j])] * jnp.exp(cum[j] - wl[j] - m[j])).astype(BF16) for j in n]
        rt = [(r_ref[:, lanes(ps[j])] * jnp.exp(cum[j] - m[j])).astype(BF16) for j in n]
        bt = [(b_ref[0, :, lanes(ps[j])] * e_iv[j]).astype(BF16) for j in n]
        kt = [(kd_ref[0, :, lanes(ps[j])] * e_iv[j]).astype(BF16) for j in n]
        em = [jnp.exp(m[j]) for j in n]
        et = [jnp.exp(tot[j]) for j in n]
        s0s = [(s0[j] * em[j]).astype(BF16) for j in n]
        ar = [jnp.concatenate([at[j], rt[j]], axis=0) for j in n]
        big = [_dot_nt(ar[j], jnp.concatenate([stack(bt[j], lh), stack(kt[j], lh)], axis=0))
               for j in n]
        sx = [_dot_nt(ar[j], s0s[j]) for j in n]
        top = [jnp.where(strict, big[j][:C], jnp.zeros((C, 2 * R), F32)) for j in n]
        lab = [top[j][:, :R].astype(BF16) for j in n]
        lak = [top[j][:, R:].astype(BF16) for j in n]
        arbk = [jnp.where(incl, big[j][C:], jnp.zeros((C, 2 * R), F32)).astype(BF16) for j in n]
        v_s = [stack(v[j], lh) for j in n]
        x = [sx[j][:C] + _dot(lak[j], v_s[j]) for j in n]
        for i in range(n_sq - 1):
            both = [_dot(lab[j], jnp.concatenate([stack(x[j].astype(BF16), lh), stack(lab[j], lc)], axis=1))
                    for j in n]
            x = [x[j] + both[j][:, :LANE] for j in n]
            lab = [both[j][:, LANE:].astype(BF16) for j in n]
        u = [(x[j] + _dot(lab[j], stack(x[j].astype(BF16), lh))).astype(BF16) for j in n]
        y = [sx[j][C:] + _dot(arbk[j], jnp.concatenate([stack(u[j], lh), v_s[j]], axis=0)) for j in n]
        upd = [_dot_tn(jnp.concatenate([u[j], v[j]], axis=0), jnp.concatenate([bt[j], kt[j]], axis=0))
               for j in n]
        for j in n:
            y_ref[0, :, lanes(ps[j])] = y[j]
            state[ps[j]] = s0[j] * et[j] + jnp.where(same_head, upd[j], jnp.zeros_like(upd[j])) * em[j]
        return carry

    lax.fori_loop(0, n_slabs // unroll, body, 0)


def _rwkv_scan(r, v, kk, wl, b, kd):
    S, W = r.shape
    C = CHUNK
    nc = S // C
    n_slabs = W // LANE

    def shared(d, c):
        return (c + d * (nc - 1 - 2 * c), 0)

    def perdir(d, c):
        return (d, c + d * (nc - 1 - 2 * c), 0)

    s1 = pl.BlockSpec((C, W), shared)
    s2 = pl.BlockSpec((1, C, W), perdir)
    return pl.pallas_call(
        functools.partial(_scan_kernel, C=C, n_slabs=n_slabs, hs=HEAD_SIZE,
                          unroll=math.gcd(n_slabs, SCAN_UNROLL)),
        grid=(2, nc),
        in_specs=[s1, s1, s1, s2, s2, s2],
        out_specs=s2,
        out_shape=jax.ShapeDtypeStruct((2, S, W), F32),
        scratch_shapes=[pltpu.VMEM((n_slabs, LANE, LANE), F32)],
        compiler_params=_params(("arbitrary", "arbitrary")),
        name="rwkv_scan",
    )(r, v, kk, wl, b, kd)


def _fin_kernel(y_ref, g_ref, bg_ref, lg_ref, lb_ref, bd_ref, o_ref, *, hs):
    y = y_ref[0] + y_ref[1]
    bd = bd_ref[...]
    mean = _seg_sum(y, bd) * (1.0 / hs)
    yc = y - mean
    var = _seg_sum(yc * yc, bd) * (1.0 / hs)
    yn = yc * lax.rsqrt(var + GN_EPS) * lg_ref[...] + lb_ref[...]
    o_ref[...] = (yn * g_ref[...] + bg_ref[...]).astype(o_ref.dtype)


def _rwkv_finish(y2, g, bg, lw, T, cw):
    _, S, W = y2.shape
    one = pl.BlockSpec((T, cw), lambda i, j: (i, j))
    vec = pl.BlockSpec((1, cw), lambda i, j: (0, j))
    return pl.pallas_call(
        functools.partial(_fin_kernel, hs=HEAD_SIZE),
        grid=(S // T, W // cw),
        in_specs=[pl.BlockSpec((2, T, cw), lambda i, j: (0, i, j)), one, one, vec, vec,
                  pl.BlockSpec((3 * LANE, LANE), lambda i, j: (0, 0))],
        out_specs=one,
        out_shape=jax.ShapeDtypeStruct((S, W), BF16),
        compiler_params=_params(("parallel", "parallel")),
        name="rwkv_finish",
    )(y2, g, bg, lw["lnx_g"], lw["lnx_b"], lw["bd"])


def _dft1_kernel(x_ref, cs_ref, tc_ref, ts_ref, o_ref, *, n1):
    g = _dot(cs_ref[...], x_ref[...])
    gr, gi = g[:n1], g[n1:]
    reps = x_ref.shape[1] // LANE
    tc = jnp.concatenate([tc_ref[0]] * reps, axis=1)
    ts = jnp.concatenate([ts_ref[0]] * reps, axis=1)
    o_ref[0] = (gr * tc + gi * ts).astype(o_ref.dtype)
    o_ref[1] = (gi * tc - gr * ts).astype(o_ref.dtype)


def _dft2_kernel(g_ref, m2_ref, cc_ref, sc_ref, o_ref, *, n2, gw):
    fw = g_ref.shape[-1]
    g = g_ref[...].reshape(2 * n2, fw)
    ab = _dot(m2_ref[...], g)
    a = ab[:n2].astype(BF16)
    b = ab[n2:].astype(BF16)
    cc = cc_ref[...]
    sc = sc_ref[...]
    for q in range(fw // gw):
        sl = slice(q * gw, (q + 1) * gw)
        o_ref[:, sl] = (_dot(a[:, sl], cc) + _dot(b[:, sl], sc)).astype(o_ref.dtype)


def _fourier_tables(S, gw):
    n1 = 1 << (int(math.log2(S)) // 2)
    n2 = S // n1
    assert n1 * n2 == S
    i1 = np.arange(n1)
    ang1 = 2.0 * np.pi * np.outer(i1, i1) / n1
    cs1 = np.concatenate([np.cos(ang1), -np.sin(ang1)], axis=0)
    i2 = np.arange(n2)
    angt = 2.0 * np.pi * np.outer(i2, i1) / S
    ang2 = 2.0 * np.pi * np.outer(i2, i2) / n2
    c2, s2 = np.cos(ang2), np.sin(ang2)
    m2 = np.block([[c2, s2], [-s2, c2]])
    ic = np.arange(gw)
    angc = 2.0 * np.pi * np.outer(ic, ic) / gw
    scale = 1.0 / math.sqrt(S * gw)
    return dict(n1=n1, n2=n2,
                cs1=jnp.asarray(cs1, BF16),
                tc=jnp.asarray(np.cos(angt), F32), ts=jnp.asarray(np.sin(angt), F32),
                m2=jnp.asarray(m2, BF16),
                cc=jnp.asarray(np.cos(angc) * scale, BF16), sc=jnp.asarray(np.sin(angc) * scale, BF16))


def _fourier_mixer(pf, tab):
    S, Fw = pf.shape
    n1, n2 = tab["n1"], tab["n2"]
    gw = Fw // N_FOURIER_GROUPS
    tc = jnp.broadcast_to(tab["tc"][:, :, None], (n2, n1, LANE))
    ts = jnp.broadcast_to(tab["ts"][:, :, None], (n2, n1, LANE))
    x2 = pf.reshape(n1, n2 * Fw)
    g = pl.pallas_call(
        functools.partial(_dft1_kernel, n1=n1),
        grid=(n2,),
        in_specs=[pl.BlockSpec((n1, Fw), lambda j: (0, j)),
                  pl.BlockSpec((2 * n1, n1), lambda j: (0, 0)),
                  pl.BlockSpec((1, n1, LANE), lambda j: (j, 0, 0)),
                  pl.BlockSpec((1, n1, LANE), lambda j: (j, 0, 0))],
        out_specs=pl.BlockSpec((2, n1, Fw), lambda j: (0, 0, j)),
        out_shape=jax.ShapeDtypeStruct((2, n1, n2 * Fw), BF16),
        compiler_params=_params(("parallel",)),
        name="dft_stage1",
    )(x2, tab["cs1"], tc, ts)
    g4 = g.reshape(2, n1, n2, Fw)
    out = pl.pallas_call(
        functools.partial(_dft2_kernel, n2=n2, gw=gw),
        grid=(n1,),
        in_specs=[pl.BlockSpec((2, None, n2, Fw), lambda i: (0, i, 0, 0)),
                  pl.BlockSpec((2 * n2, 2 * n2), lambda i: (0, 0)),
                  pl.BlockSpec((gw, gw), lambda i: (0, 0)),
                  pl.BlockSpec((gw, gw), lambda i: (0, 0))],
        out_specs=pl.BlockSpec((n2, Fw), lambda i: (0, i)),
        out_shape=jax.ShapeDtypeStruct((n2, n1 * Fw), BF16),
        compiler_params=_params(("parallel",)),
        name="dft_stage2",
    )(g4, tab["m2"], tab["cc"], tab["sc"])
    return out.reshape(S, Fw)


def _acc_dot(o_ref, x, w_ref):
    D = o_ref.shape[1]
    pn = _pick(D, 512)
    for n in range(0, D, pn):
        o_ref[:, n:n + pn] += _dot(x, w_ref[:, n:n + pn])


def _res_ln_finish(k, h_ref, g_ref, b_ref, o_ref, ob_ref, alpha):
    @pl.when(k == pl.num_programs(1) - 1)
    def _():
        T = o_ref.shape[0]
        for r in range(0, T, 16):
            for r8 in (r, r + 8):
                rows = pl.ds(r8, 8)
                o_ref[rows, :] = _ln_rows(alpha * h_ref[rows, :] + o_ref[rows, :], g_ref[...], b_ref[...])
            ob_ref[pl.ds(r, 16), :] = o_ref[pl.ds(r, 16), :].astype(BF16)


def _proj_ln_kernel(xa_ref, xb_ref, wa_ref, wb_ref, h_ref, g_ref, b_ref, o_ref, ob_ref, *, alpha, tn):
    n = pl.program_id(1)
    cols = pl.ds(pl.multiple_of(n * tn, LANE), tn)
    o_ref[:, cols] = _dot(xa_ref[...], wa_ref[...]) + _dot(xb_ref[...], wb_ref[...])
    _res_ln_finish(n, h_ref, g_ref, b_ref, o_ref, ob_ref, alpha)


def _ffn_tail_kernel(ug, uv, lg, lv, fg, fv, cg, cv, bg, bv, w_ref, h_ref, g_ref, b_ref, o_ref, ob_ref,
                     act_a, act_b, *, alpha):
    k = pl.program_id(1)

    @pl.when(k == 0)
    def _():
        o_ref[...] = jnp.zeros_like(o_ref)
        act_a[...] = jnp.zeros_like(act_a)

    T, D = o_ref.shape
    tk = act_a.shape[1]
    pn = _pick(D, 512)
    n_pan = D // pn
    rb = _pick(T, 64, 16)
    i = pl.program_id(0)
    first_tile = i == 0
    last_tile = i == pl.num_programs(0) - 1
    row = lax.broadcasted_iota(jnp.int32, (rb, 1), 0)

    pk = 16

    def conv(u_ref, last_ref, first_ref, c_ref, cb_ref, r0, cols):
        u = u_ref[r0:r0 + rb, cols].astype(F32)
        if r0 == 0:
            up = jnp.where(first_tile, 0.0, last_ref[0, :, cols])
        else:
            up = u_ref[r0 - pk:r0, cols].astype(F32)[pk - 1:pk]
        if r0 + rb == T:
            un = jnp.where(last_tile, 0.0, first_ref[0, :, cols])
        else:
            un = u_ref[r0 + rb:r0 + rb + pk, cols].astype(F32)[0:1]
        prev = jnp.where(row == 0, up, pltpu.roll(u, 1, 0))
        nxt = jnp.where(row == rb - 1, un, pltpu.roll(u, rb - 1, 0))
        return (c_ref[0:1, cols] * prev + c_ref[1:2, cols] * u + c_ref[2:3, cols] * nxt
                + cb_ref[:, cols])

    blocks = [(r0, slice(c, c + LANE)) for c in range(0, tk, LANE) for r0 in range(0, T, rb)]

    def step(act_in, act_out):
        for q in range(n_pan):
            n = q * pn
            o_ref[:, n:n + pn] += _dot(act_in[...], w_ref[:, n:n + pn])
            for r0, cols in blocks[q * len(blocks) // n_pan:(q + 1) * len(blocks) // n_pan]:
                gate = conv(ug, lg, fg, cg, bg, r0, cols)
                val = conv(uv, lv, fv, cv, bv, r0, cols)
                gelu = 0.5 * gate * (1.0 + lax.erf(gate * (1.0 / math.sqrt(2.0))))
                act_out[r0:r0 + rb, cols] = (gelu * val).astype(BF16)

    @pl.when(k % 2 == 0)
    def _():
        step(act_a, act_b)

    @pl.when(k % 2 == 1)
    def _():
        step(act_b, act_a)

    _res_ln_finish(k, h_ref, g_ref, b_ref, o_ref, ob_ref, alpha)


def _row_tile_specs(T, D):
    h_spec = pl.BlockSpec((T, D), lambda i, k: (i, 0), pipeline_mode=pl.Buffered(1))
    vec = pl.BlockSpec((8, D), lambda i, k: (0, 0))
    out = pl.BlockSpec((T, D), lambda i, k: (i, 0))
    return h_spec, vec, out


def _proj_ln(xa, xb, w, h, g, b, alpha, T=512, tn=512, name="proj_ln"):
    M, Ka = xa.shape
    Kb = xb.shape[1]
    D = w.shape[1]
    assert Ka == Kb and w.shape[0] == Ka + Kb
    T = _pick(M, T, 8)
    tn = _pick(D, tn)
    h_spec, vec, out = _row_tile_specs(T, D)

    def x_spec(K):
        return pl.BlockSpec((T, K), lambda i, n: (i, 0), pipeline_mode=pl.Buffered(1))

    return pl.pallas_call(
        functools.partial(_proj_ln_kernel, alpha=alpha, tn=tn),
        grid=(M // T, D // tn),
        in_specs=[x_spec(Ka), x_spec(Kb),
                  pl.BlockSpec((Ka, tn), lambda i, n: (0, n)),
                  pl.BlockSpec((Kb, tn), lambda i, n: (1, n)), h_spec, vec, vec],
        out_specs=(out, out),
        out_shape=(jax.ShapeDtypeStruct((M, D), F32), jax.ShapeDtypeStruct((M, D), BF16)),
        compiler_params=_params(("parallel", "arbitrary")),
        name=name,
    )(xa, xb, w, w, h, jnp.broadcast_to(g, (8, D)), jnp.broadcast_to(b, (8, D)))


def _ffn_tail(u, u_first, u_last, conv_w, conv_b, w_down, h, g, b, alpha, T, tk=512):
    S, N2 = u.shape
    dffp = N2 // 2
    D = w_down.shape[1]
    tk = _pick(dffp, tk)
    nk = dffp // tk
    nb = S // T
    h_spec, vec, out = _row_tile_specs(T, D)

    def made(k):
        return jnp.minimum(k, nk - 1)

    def main(off):
        return pl.BlockSpec((T, tk), lambda i, k: (i, made(k) + off * nk))

    def cvec(off, rows):
        return pl.BlockSpec((rows, tk), lambda i, k: (0, made(k) + off * nk))

    hg = _halo_specs(nb, tk, made)
    hv = _halo_specs(nb, tk, lambda k: made(k) + nk)
    return pl.pallas_call(
        functools.partial(_ffn_tail_kernel, alpha=alpha),
        grid=(nb, nk + 1),
        in_specs=[main(0), main(1), hg[0], hv[0], hg[1], hv[1],
                  cvec(0, 3), cvec(1, 3), cvec(0, 1), cvec(1, 1),
                  pl.BlockSpec((tk, D), lambda i, k: (jnp.maximum(k - 1, 0), 0)), h_spec, vec, vec],
        out_specs=(out, out),
        out_shape=(jax.ShapeDtypeStruct((S, D), F32), jax.ShapeDtypeStruct((S, D), BF16)),
        scratch_shapes=[pltpu.VMEM((T, tk), BF16), pltpu.VMEM((T, tk), BF16)],
        compiler_params=_params(("parallel", "arbitrary")),
        name="ffn_tail",
    )(u, u, u_last, u_last, u_first, u_first, conv_w, conv_w, conv_b, conv_b, w_down, h,
      jnp.broadcast_to(g, (8, D)), jnp.broadcast_to(b, (8, D)))


def _layer_weights(l, W, gl, dl, al, w_in, mu_shift, decay_base, w_decay_up, aaa_base, w_aaa_up,
                   g_lora_up, k_k, k_a, r_k, lnx_g, lnx_b):
    gp = _round_up(gl, LANE)
    lp = _round_up(max(dl, al), LANE)
    o = 3 * W

    def padc(a, n):
        return jnp.pad(a, ((0, 0), (0, n - a.shape[1])))

    def padr(a, n):
        return jnp.pad(a, ((0, 0), (0, n - a.shape[1]), (0, 0)))

    wl, ml = w_in[l], mu_shift[l][None, :]
    segs = [(o, gl, gp), (o + gl, dl, lp), (o + gl + dl, dl, lp),
            (o + gl + 2 * dl, al, lp), (o + gl + 2 * dl + al, al, lp)]
    w_lora = jnp.concatenate([padc(wl[:, s:s + n], p) for s, n, p in segs], axis=1)
    mu_lora = jnp.concatenate([padc(ml[:, s:s + n], p) for s, n, p in segs], axis=1)
    n_shift = o + gl + 2 * dl + 2 * al
    hs = HEAD_SIZE
    idx = np.arange(LANE)
    bd = jnp.asarray(np.tile((idx[:, None] // hs) == (idx[None, :] // hs), (3, 1)), BF16)
    return dict(
        gp=gp, lp=lp,
        w_rkv=wl[:, :o].astype(BF16), w_lora=w_lora.astype(BF16), w_f=wl[:, n_shift:].astype(BF16),
        mu_rkv=ml[:, :o], mu_lora=mu_lora,
        decay_base=decay_base[l], aaa_base=aaa_base[l],
        w_decay_up=padr(w_decay_up[l], lp).astype(BF16), w_aaa_up=padr(w_aaa_up[l], lp).astype(BF16),
        g_up=jnp.pad(g_lora_up[l], ((0, gp - gl), (0, 0))).astype(BF16),
        k_k=k_k[l][None, :], k_a=k_a[l][None, :], r_k=r_k[l].reshape(1, W),
        lnx_g=lnx_g[l][None, :], lnx_b=lnx_b[l][None, :], bd=bd)


def kernel(x, ln0_g, ln0_b, w_in, mu_shift, decay_base, w_decay_up, aaa_base, w_aaa_up, g_lora_up, k_k, k_a, r_k, lnx_g, lnx_b, w_out, ln1_g, ln1_b, w_ffn_up, ffn_conv_w, ffn_conv_b, w_ffn_down, ln2_g, ln2_b):
    B, S, D = x.shape
    depth = w_in.shape[0]
    W = k_k.shape[1]
    gl, dl, al = g_lora_up.shape[1], w_decay_up.shape[2], w_aaa_up.shape[2]
    n_shift = 3 * W + gl + 2 * dl + 2 * al
    Fw = w_in.shape[2] - n_shift
    dff = w_ffn_down.shape[1]
    alpha = float((2 * depth) ** 0.25)
    assert B == 1 and S % CHUNK == 0 and W % LANE == 0 and LANE % HEAD_SIZE == 0

    tab = _fourier_tables(S, Fw // N_FOURIER_GROUPS)
    Tp = _pick(S, 256, 8)
    cwp = _pick(W, 512)
    Tf = _pick(S, 512, 8)
    dffp = _round_up(dff, 512)
    pad = dffp - dff

    def pad_gv(a):
        cfg = [(0, 0)] * (a.ndim - 1) + [(0, pad)]
        return jnp.concatenate([jnp.pad(a[..., :dff], cfg), jnp.pad(a[..., dff:], cfg)], axis=-1)

    h, hb = _layer_norm(x.reshape(S, D), ln0_g, ln0_b)
    for l in range(depth):
        lw = _layer_weights(l, W, gl, dl, al, w_in, mu_shift, decay_base, w_decay_up, aaa_base,
                            w_aaa_up, g_lora_up, k_k, k_a, r_k, lnx_g, lnx_b)
        prkv, rkv_first, rkv_last = _matmul(hb, lw["w_rkv"], F32, name="in_rkv", edge_rows=Tp)
        plora, lora_first, lora_last = _matmul(hb, lw["w_lora"], F32, name="in_lora", edge_rows=Tp)
        pf = _matmul(hb, lw["w_f"], BF16, name="in_fourier")
        r, v, kk, wl, b, kd, g, bg = _rwkv_prep(prkv, rkv_first, rkv_last, plora, lora_first, lora_last,
                                                lw, Tp, cwp)
        y2 = _rwkv_scan(r, v, kk, wl, b, kd)
        y_rwkv = _rwkv_finish(y2, g, bg, lw, Tf, cwp)
        y_four = _fourier_mixer(pf, tab)
        h, hb = _proj_ln(y_rwkv, y_four, w_out[l].astype(BF16), h, ln1_g[l], ln1_b[l], alpha, T=Tf,
                         name="out_proj_ln")
        u, u_first, u_last = _matmul(hb, pad_gv(w_ffn_up[l]).astype(BF16), BF16, name="ffn_up", edge_rows=Tf)
        w_down = jnp.pad(w_ffn_down[l], ((0, pad), (0, 0))).astype(BF16)
        h, hb = _ffn_tail(u, u_first, u_last, pad_gv(ffn_conv_w[l]), pad_gv(ffn_conv_b[l][None, :]),
                          w_down, h, ln2_g[l], ln2_b[l], alpha, Tf)
    return h.reshape(B, S, D)
```

```python
import functools
import math

import numpy as np
import jax
import jax.numpy as jnp
from jax import lax
from jax.experimental import pallas as pl
from jax.experimental.pallas import tpu as pltpu

F32 = jnp.float32
BF16 = jnp.bfloat16

LANE = 128
HEAD_SIZE = 64
N_FOURIER_GROUPS = 8
LN_EPS = 1e-5
GN_EPS = 64e-5
CHUNK = 64
SCAN_UNROLL = 16
VMEM_LIMIT = 56 * 1024 * 1024


def _round_up(n, m):
    return (n + m - 1) // m * m


def _pick(dim, target, align=LANE):
    if dim <= target:
        return dim
    best = None
    for t in range(align, target + 1, align):
        if dim % t == 0:
            best = t
    assert best is not None, (dim, target, align)
    return best


def _params(sem):
    return pltpu.CompilerParams(dimension_semantics=sem, vmem_limit_bytes=VMEM_LIMIT)


def _mm_kernel(x_ref, w_ref, o_ref):
    o_ref[...] = jnp.dot(x_ref[...], w_ref[...], preferred_element_type=F32).astype(o_ref.dtype)


def _mm_edges_kernel(x_ref, w_ref, o_ref, f_ref, l_ref, *, tb):
    res = jnp.dot(x_ref[...], w_ref[...], preferred_element_type=F32)
    o_ref[...] = res.astype(o_ref.dtype)
    for q in range(res.shape[0] // tb):
        f_ref[q] = res[q * tb:q * tb + 1, :]
        l_ref[q] = res[(q + 1) * tb - 1:(q + 1) * tb, :]


def _matmul(x, w, out_dtype, tm=1024, tn=1024, tk=4096, name="mm", edge_rows=None):
    M, K = x.shape
    N = w.shape[1]
    tm = _pick(M, tm, 8)
    tn = _pick(N, tn)
    tk = _pick(K, tk)
    if edge_rows is not None:
        assert tk == K and tm % edge_rows == 0
        nq = tm // edge_rows
        edge = pl.BlockSpec((nq, 1, tn), lambda j, i: (i, 0, j))
        esd = jax.ShapeDtypeStruct((M // edge_rows, 1, N), F32)
        return pl.pallas_call(
            functools.partial(_mm_edges_kernel, tb=edge_rows),
            grid=(N // tn, M // tm),
            in_specs=[pl.BlockSpec((tm, K), lambda j, i: (i, 0)),
                      pl.BlockSpec((K, tn), lambda j, i: (0, j))],
            out_specs=(pl.BlockSpec((tm, tn), lambda j, i: (i, j)), edge, edge),
            out_shape=(jax.ShapeDtypeStruct((M, N), out_dtype), esd, esd),
            compiler_params=_params(("parallel", "parallel")),
            name=name,
        )(x, w)
    assert tk == K
    return pl.pallas_call(
        _mm_kernel,
        grid=(N // tn, M // tm),
        in_specs=[pl.BlockSpec((tm, K), lambda j, i: (i, 0)),
                  pl.BlockSpec((K, tn), lambda j, i: (0, j))],
        out_specs=pl.BlockSpec((tm, tn), lambda j, i: (i, j)),
        out_shape=jax.ShapeDtypeStruct((M, N), out_dtype),
        compiler_params=_params(("parallel", "parallel")),
        name=name,
    )(x, w)


BF16_ROWS = 16


def _mm_shift_kernel(x_ref, xp_ref, xn_ref, w_ref, mu_ref, o_ref, *, pn):
    i = pl.program_id(1)
    tm = x_ref.shape[0]
    xh = jnp.concatenate([xp_ref[...], xn_ref[...]], axis=0)
    def panel(n):
        cols = slice(n, n + pn)
        return _dot(x_ref[...], w_ref[:, cols]), _dot(xh, w_ref[:, cols])

    starts = list(range(0, o_ref.shape[1], pn))
    nxt_panel = panel(starts[0])
    for q, n in enumerate(starts):
        cols = slice(n, n + pn)
        p, ph = nxt_panel
        if q + 1 < len(starts):
            nxt_panel = panel(starts[q + 1])
        mu = mu_ref[:, cols]
        zp = jnp.where(i > 0, ph[BF16_ROWS - 1:BF16_ROWS], 0.0)
        zn = jnp.where(i < pl.num_programs(1) - 1, ph[BF16_ROWS:BF16_ROWS + 1], 0.0)
        o_ref[:, cols] = p + (0.5 * (pltpu.roll(p, 1, 0) + pltpu.roll(p, tm - 1, 0)) - p) * mu
        first, last = p[0:1], p[tm - 1:tm]
        o_ref[0:1, cols] = first + (0.5 * (zp + p[1:2]) - first) * mu
        o_ref[tm - 1:tm, cols] = last + (0.5 * (p[tm - 2:tm - 1] + zn) - last) * mu


def _matmul_shift(x, w, mu, tm=1024, tn=1024, name="mm_shift"):
    M, K = x.shape
    N = w.shape[1]
    tm = _pick(M, tm, BF16_ROWS)
    tn = _pick(N, tn)
    nh = tm // BF16_ROWS
    last = M // BF16_ROWS - 1
    return pl.pallas_call(
        functools.partial(_mm_shift_kernel, pn=_pick(tn, 256)),
        grid=(N // tn, M // tm),
        in_specs=[pl.BlockSpec((tm, K), lambda j, i: (i, 0)),
                  pl.BlockSpec((BF16_ROWS, K), lambda j, i: (jnp.maximum(i * nh - 1, 0), 0)),
                  pl.BlockSpec((BF16_ROWS, K), lambda j, i: (jnp.minimum((i + 1) * nh, last), 0)),
                  pl.BlockSpec((K, tn), lambda j, i: (0, j)),
                  pl.BlockSpec((1, tn), lambda j, i: (0, j))],
        out_specs=pl.BlockSpec((tm, tn), lambda j, i: (i, j)),
        out_shape=jax.ShapeDtypeStruct((M, N), F32),
        compiler_params=_params(("parallel", "parallel")),
        name=name,
    )(x, x, x, w, mu)


def _ln_rows(x, g, b):
    mu = jnp.mean(x, axis=-1, keepdims=True)
    xc = x - mu
    var = jnp.mean(xc * xc, axis=-1, keepdims=True)
    return xc * lax.rsqrt(var + LN_EPS) * g + b


def _ln_kernel(x_ref, g_ref, b_ref, o_ref, ob_ref):
    y = _ln_rows(x_ref[...], g_ref[...], b_ref[...])
    o_ref[...] = y
    ob_ref[...] = y.astype(BF16)


def _layer_norm(x, g, b):
    M, D = x.shape
    tm = _pick(M, 256, 8)
    row = pl.BlockSpec((tm, D), lambda i: (i, 0))
    vec = pl.BlockSpec((1, D), lambda i: (0, 0))
    return pl.pallas_call(
        _ln_kernel, grid=(M // tm,), in_specs=[row, vec, vec], out_specs=(row, row),
        out_shape=(jax.ShapeDtypeStruct((M, D), F32), jax.ShapeDtypeStruct((M, D), BF16)),
        compiler_params=_params(("parallel",)), name="ln",
    )(x, g.reshape(1, D), b.reshape(1, D))


def _halo_specs(T_blocks, cw, col):
    prev = pl.BlockSpec((1, 1, cw), lambda i, j: (jnp.maximum(i - 1, 0), 0, col(j)))
    nxt = pl.BlockSpec((1, 1, cw), lambda i, j: (jnp.minimum(i + 1, T_blocks - 1), 0, col(j)))
    return prev, nxt


def _split3(w):
    w1 = w.astype(BF16)
    res = w - w1.astype(F32)
    w2 = res.astype(BF16)
    w3 = (res - w2.astype(F32)).astype(BF16)
    return jnp.concatenate([w1, w2, w3], axis=1)


def _seg_sum(x, bd3):
    parts = [jnp.dot(_split3(x[:, i * LANE:(i + 1) * LANE]), bd3, preferred_element_type=F32)
             for i in range(x.shape[1] // LANE)]
    return parts[0] if len(parts) == 1 else jnp.concatenate(parts, axis=1)


def _sigmoid(x):
    return 1.0 / (1.0 + jnp.exp(-x))


def _prep_kernel(zr, zk, zv, zlo, dbase, wdu, abase, wau, gup, kkw, kaw, rkw, bd,
                 r_o, v_o, kk_o, wl_o, b_o, kd_o, g_o, bg_o, *, gp, lp):
    r = zr[...]
    k = zk[...]
    v = zv[...]
    zl = zlo[...]
    bdm = bd[...]

    g = jnp.dot(_sigmoid(zl[:, :gp]).astype(BF16), gup[...], preferred_element_type=F32)
    kk = k * kkw[...]
    kk = kk / jnp.maximum(jnp.sqrt(_seg_sum(kk * kk, bdm)), 1e-12)
    ka = kaw[...]
    ksum = jnp.zeros_like(k)
    for d in range(2):
        wd = jnp.tanh(zl[:, gp + d * lp:gp + (d + 1) * lp]).astype(BF16)
        x = dbase[d:d + 1, :] + jnp.dot(wd, wdu[d], preferred_element_type=F32)
        wl_o[d] = -math.exp(-0.5) * _sigmoid(x)
        ad = zl[:, gp + (2 + d) * lp:gp + (3 + d) * lp].astype(BF16)
        a = _sigmoid(abase[d:d + 1, :] + jnp.dot(ad, wau[d], preferred_element_type=F32))
        kd = k * (1.0 + (a - 1.0) * ka)
        kd_o[d] = kd.astype(kd_o.dtype)
        b_o[d] = (a * kk).astype(b_o.dtype)
        ksum = ksum + kd
    bonus = _seg_sum(r * ksum * rkw[...], bdm) * v
    r_o[...] = r.astype(r_o.dtype)
    v_o[...] = v.astype(v_o.dtype)
    kk_o[...] = kk.astype(kk_o.dtype)
    g_o[...] = g.astype(g_o.dtype)
    bg_o[...] = (bonus * g).astype(bg_o.dtype)


def _rwkv_prep(zrkv, zlora, lw, T, cw):
    S = zrkv.shape[0]
    W = zrkv.shape[1] // 3
    LW = zlora.shape[1]
    gp, lp = lw["gp"], lw["lp"]
    nb, ncb = S // T, W // cw

    def main(off):
        return pl.BlockSpec((T, cw), lambda i, j: (i, j + off * ncb))

    def vec(rows=1):
        return pl.BlockSpec((rows, cw), lambda i, j: (0, j))

    in_specs = [main(0), main(1), main(2),
                pl.BlockSpec((T, LW), lambda i, j: (i, 0)),
                vec(2),
                pl.BlockSpec((2, lp, cw), lambda i, j: (0, 0, j)),
                vec(2),
                pl.BlockSpec((2, lp, cw), lambda i, j: (0, 0, j)),
                pl.BlockSpec((gp, cw), lambda i, j: (0, j)),
                vec(), vec(), vec(),
                pl.BlockSpec((3 * LANE, LANE), lambda i, j: (0, 0))]
    one = pl.BlockSpec((T, cw), lambda i, j: (i, j))
    two = pl.BlockSpec((2, T, cw), lambda i, j: (0, i, j))
    sd1 = jax.ShapeDtypeStruct((S, W), BF16)
    sd2 = jax.ShapeDtypeStruct((2, S, W), BF16)
    return pl.pallas_call(
        functools.partial(_prep_kernel, gp=gp, lp=lp),
        grid=(nb, ncb),
        in_specs=in_specs,
        out_specs=(one, one, one, two, two, two, one, one),
        out_shape=(sd1, sd1, sd1, jax.ShapeDtypeStruct((2, S, W), F32), sd2, sd2, sd1, sd1),
        compiler_params=_params(("parallel", "parallel")),
        name="rwkv_prep",
    )(zrkv, zrkv, zrkv, zlora, lw["decay_base"], lw["w_decay_up"],
      lw["aaa_base"], lw["w_aaa_up"], lw["g_up"], lw["k_k"], lw["k_a"], lw["r_k"], lw["bd"])


def _dot_nt(a, b):
    return lax.dot_general(a, b, (((1,), (1,)), ((), ())), preferred_element_type=F32)


def _dot_tn(a, b):
    return lax.dot_general(a, b, (((0,), (0,)), ((), ())), preferred_element_type=F32)


def _dot(a, b):
    return jnp.dot(a, b, preferred_element_type=F32)


def _scan_kernel(r_ref, v_ref, kk_ref, wl_ref, b_ref, kd_ref, y_ref,
                 state, *, C, n_slabs, hs, unroll):
    d = pl.program_id(0)
    c = pl.program_id(1)

    @pl.when(c == 0)
    def _():
        state[...] = jnp.zeros_like(state)

    sgn = 1 - 2 * d
    ri = lax.broadcasted_iota(jnp.int32, (C, C), 0)
    ci = lax.broadcasted_iota(jnp.int32, (C, C), 1)
    tri = ((ri - ci) * sgn >= 0).astype(BF16)

    n_heads = LANE // hs
    R = n_heads * C
    lc = int(math.log2(C))
    lh = int(math.log2(hs))
    n_sq = lc
    ri = lax.broadcasted_iota(jnp.int32, (C, 2 * R), 0)
    ci = lax.broadcasted_iota(jnp.int32, (C, 2 * R), 1)
    diff = (ri - (ci & (C - 1))) * sgn
    strict = diff > 0
    incl = diff >= 0
    sr = lax.broadcasted_iota(jnp.int32, (LANE, LANE), 0)
    sc = lax.broadcasted_iota(jnp.int32, (LANE, LANE), 1)
    same_head = (sr >> lh) == (sc >> lh)

    def stack(x, shift):
        grp = lax.broadcasted_iota(jnp.int32, x.shape, 1) >> shift
        return jnp.concatenate(
            [jnp.where(grp == h, x, jnp.zeros_like(x)) for h in range(n_heads)], axis=0)

    def lanes(p):
        return pl.ds(pl.multiple_of(p * LANE, LANE), LANE)

    def body(q, carry):
        ps = [q * unroll + j for j in range(unroll)]
        n = range(unroll)
        wl = [wl_ref[0, :, lanes(p)] for p in ps]
        s0 = [state[p] for p in ps]
        v = [v_ref[:, lanes(p)].astype(BF16) for p in ps]
        cum3 = [_dot(tri, _split3(wl[j])) for j in n]
        cum = [cum3[j][:, :LANE] + cum3[j][:, LANE:2 * LANE] + cum3[j][:, 2 * LANE:] for j in n]
        tot = [jnp.sum(wl[j], axis=0, keepdims=True) for j in n]
        m = [0.5 * tot[j] for j in n]
        e_iv = [jnp.exp(m[j] - cum[j]) for j in n]
        at = [(-kk_ref[:, lanes(ps[j])] * jnp.exp(cum[j] - wl[j] - m[j])).astype(BF16) for j in n]
        rt = [(r_ref[:, lanes(ps[j])] * jnp.exp(cum[j] - m[j])).astype(BF16) for j in n]
        bt = [(b_ref[0, :, lanes(ps[j])] * e_iv[j]).astype(BF16) for j in n]
        kt = [(kd_ref[0, :, lanes(ps[j])] * e_iv[j]).astype(BF16) for j in n]
        em = [jnp.exp(m[j]) for j in n]
        et = [jnp.exp(tot[j]) for j in n]
        s0s = [(s0[j] * em[j]).astype(BF16) for j in n]
        ar = [jnp.concatenate([at[j], rt[j]], axis=0) for j in n]
        big = [_dot_nt(ar[j], jnp.concatenate([stack(bt[j], lh), stack(kt[j], lh)], axis=0))
               for j in n]
        sx = [_dot_nt(ar[j], s0s[j]) for j in n]
        top = [jnp.where(strict, big[j][:C], jnp.zeros((C, 2 * R), F32)) for j in n]
        lab = [top[j][:, :R].astype(BF16) for j in n]
        lak = [top[j][:, R:].astype(BF16) for j in n]
        arbk = [jnp.where(incl, big[j][C:], jnp.zeros((C, 2 * R), F32)).astype(BF16) for j in n]
        v_s = [stack(v[j], lh) for j in n]
        x = [sx[j][:C] + _dot(lak[j], v_s[j]) for j in n]
        for i in range(n_sq - 1):
            both = [_dot(lab[j], jnp.concatenate([stack(x[j].astype(BF16), lh), stack(lab[j], lc)], axis=1))
                    for j in n]
            x = [x[j] + both[j][:, :LANE] for j in n]
            lab = [both[j][:, LANE:].astype(BF16) for j in n]
        u = [(x[j] + _dot(lab[j], stack(x[j].astype(BF16), lh))).astype(BF16) for j in n]
        y = [sx[j][C:] + _dot(arbk[j], jnp.concatenate([stack(u[j], lh), v_s[j]], axis=0)) for j in n]
        upd = [_dot_tn(jnp.concatenate([u[j], v[j]], axis=0), jnp.concatenate([bt[j], kt[j]], axis=0))
               for j in n]
        for j in n:
            y_ref[0, :, lanes(ps[j])] = y[j]
            state[ps[j]] = s0[j] * et[j] + jnp.where(same_head, upd[j], jnp.zeros_like(upd[j])) * em[j]
        return carry

    lax.fori_loop(0, n_slabs // unroll, body, 0)


def _rwkv_scan(r, v, kk, wl, b, kd):
    S, W = r.shape
    C = CHUNK
    nc = S // C
    n_slabs = W // LANE

    def shared(d, c):
        return (c + d * (nc - 1 - 2 * c), 0)

    def perdir(d, c):
        return (d, c + d * (nc - 1 - 2 * c), 0)

    s1 = pl.BlockSpec((C, W), shared)
    s2 = pl.BlockSpec((1, C, W), perdir)
    return pl.pallas_call(
        functools.partial(_scan_kernel, C=C, n_slabs=n_slabs, hs=HEAD_SIZE,
                          unroll=math.gcd(n_slabs, SCAN_UNROLL)),
        grid=(2, nc),
        in_specs=[s1, s1, s1, s2, s2, s2],
        out_specs=s2,
        out_shape=jax.ShapeDtypeStruct((2, S, W), F32),
        scratch_shapes=[pltpu.VMEM((n_slabs, LANE, LANE), F32)],
        compiler_params=_params(("arbitrary", "arbitrary")),
        name="rwkv_scan",
    )(r, v, kk, wl, b, kd)


def _fin_kernel(y_ref, g_ref, bg_ref, lg_ref, lb_ref, bd_ref, o_ref, *, hs):
    y = y_ref[0] + y_ref[1]
    bd = bd_ref[...]
    mean = _seg_sum(y, bd) * (1.0 / hs)
    yc = y - mean
    var = _seg_sum(yc * yc, bd) * (1.0 / hs)
    yn = yc * lax.rsqrt(var + GN_EPS) * lg_ref[...] + lb_ref[...]
    o_ref[...] = (yn * g_ref[...] + bg_ref[...]).astype(o_ref.dtype)


def _rwkv_finish(y2, g, bg, lw, T, cw):
    _, S, W = y2.shape
    one = pl.BlockSpec((T, cw), lambda i, j: (i, j))
    vec = pl.BlockSpec((1, cw), lambda i, j: (0, j))
    return pl.pallas_call(
        functools.partial(_fin_kernel, hs=HEAD_SIZE),
        grid=(S // T, W // cw),
        in_specs=[pl.BlockSpec((2, T, cw), lambda i, j: (0, i, j)), one, one, vec, vec,
                  pl.BlockSpec((3 * LANE, LANE), lambda i, j: (0, 0))],
        out_specs=one,
        out_shape=jax.ShapeDtypeStruct((S, W), BF16),
        compiler_params=_params(("parallel", "parallel")),
        name="rwkv_finish",
    )(y2, g, bg, lw["lnx_g"], lw["lnx_b"], lw["bd"])


def _dft1_kernel(x_ref, cs_ref, tc_ref, ts_ref, o_ref, *, n1):
    g = _dot(cs_ref[...], x_ref[...])
    gr, gi = g[:n1], g[n1:]
    reps = x_ref.shape[1] // LANE
    tc = jnp.concatenate([tc_ref[0]] * reps, axis=1)
    ts = jnp.concatenate([ts_ref[0]] * reps, axis=1)
    o_ref[0] = (gr * tc + gi * ts).astype(o_ref.dtype)
    o_ref[1] = (gi * tc - gr * ts).astype(o_ref.dtype)


def _dft2_kernel(g_ref, m2_ref, cc_ref, sc_ref, o_ref, *, n2, gw):
    fw = g_ref.shape[-1]
    g = g_ref[...].reshape(2 * n2, fw)
    ab = _dot(m2_ref[...], g)
    a = ab[:n2].astype(BF16)
    b = ab[n2:].astype(BF16)
    cc = cc_ref[...]
    sc = sc_ref[...]
    for q in range(fw // gw):
        sl = slice(q * gw, (q + 1) * gw)
        o_ref[:, sl] = (_dot(a[:, sl], cc) + _dot(b[:, sl], sc)).astype(o_ref.dtype)


def _fourier_tables(S, gw):
    n1 = 1 << (int(math.log2(S)) // 2)
    n2 = S // n1
    assert n1 * n2 == S
    i1 = np.arange(n1)
    ang1 = 2.0 * np.pi * np.outer(i1, i1) / n1
    cs1 = np.concatenate([np.cos(ang1), -np.sin(ang1)], axis=0)
    i2 = np.arange(n2)
    angt = 2.0 * np.pi * np.outer(i2, i1) / S
    ang2 = 2.0 * np.pi * np.outer(i2, i2) / n2
    c2, s2 = np.cos(ang2), np.sin(ang2)
    m2 = np.block([[c2, s2], [-s2, c2]])
    ic = np.arange(gw)
    angc = 2.0 * np.pi * np.outer(ic, ic) / gw
    scale = 1.0 / math.sqrt(S * gw)
    return dict(n1=n1, n2=n2,
                cs1=jnp.asarray(cs1, BF16),
                tc=jnp.asarray(np.cos(angt), F32), ts=jnp.asarray(np.sin(angt), F32),
                m2=jnp.asarray(m2, BF16),
                cc=jnp.asarray(np.cos(angc) * scale, BF16), sc=jnp.asarray(np.sin(angc) * scale, BF16))


def _fourier_mixer(pf, tab):
    S, Fw = pf.shape
    n1, n2 = tab["n1"], tab["n2"]
    gw = Fw // N_FOURIER_GROUPS
    tc = jnp.broadcast_to(tab["tc"][:, :, None], (n2, n1, LANE))
    ts = jnp.broadcast_to(tab["ts"][:, :, None], (n2, n1, LANE))
    x2 = pf.reshape(n1, n2 * Fw)
    g = pl.pallas_call(
        functools.partial(_dft1_kernel, n1=n1),
        grid=(n2,),
        in_specs=[pl.BlockSpec((n1, Fw), lambda j: (0, j)),
                  pl.BlockSpec((2 * n1, n1), lambda j: (0, 0)),
                  pl.BlockSpec((1, n1, LANE), lambda j: (j, 0, 0)),
                  pl.BlockSpec((1, n1, LANE), lambda j: (j, 0, 0))],
        out_specs=pl.BlockSpec((2, n1, Fw), lambda j: (0, 0, j)),
        out_shape=jax.ShapeDtypeStruct((2, n1, n2 * Fw), BF16),
        compiler_params=_params(("parallel",)),
        name="dft_stage1",
    )(x2, tab["cs1"], tc, ts)
    g4 = g.reshape(2, n1, n2, Fw)
    out = pl.pallas_call(
        functools.partial(_dft2_kernel, n2=n2, gw=gw),
        grid=(n1,),
        in_specs=[pl.BlockSpec((2, None, n2, Fw), lambda i: (0, i, 0, 0)),
                  pl.BlockSpec((2 * n2, 2 * n2), lambda i: (0, 0)),
                  pl.BlockSpec((gw, gw), lambda i: (0, 0)),
                  pl.BlockSpec((gw, gw), lambda i: (0, 0))],
        out_specs=pl.BlockSpec((n2, Fw), lambda i: (0, i)),
        out_shape=jax.ShapeDtypeStruct((n2, n1 * Fw), BF16),
        compiler_params=_params(("parallel",)),
        name="dft_stage2",
    )(g4, tab["m2"], tab["cc"], tab["sc"])
    return out.reshape(S, Fw)


def _acc_dot(o_ref, x, w_ref):
    D = o_ref.shape[1]
    pn = _pick(D, 512)
    for n in range(0, D, pn):
        o_ref[:, n:n + pn] += _dot(x, w_ref[:, n:n + pn])


def _res_ln_finish(k, h_ref, g_ref, b_ref, o_ref, ob_ref, alpha):
    @pl.when(k == pl.num_programs(1) - 1)
    def _():
        T = o_ref.shape[0]
        for r in range(0, T, 16):
            for r8 in (r, r + 8):
                rows = pl.ds(r8, 8)
                o_ref[rows, :] = _ln_rows(alpha * h_ref[rows, :] + o_ref[rows, :], g_ref[...], b_ref[...])
            ob_ref[pl.ds(r, 16), :] = o_ref[pl.ds(r, 16), :].astype(BF16)


def _proj_ln_kernel(xa_ref, xb_ref, wa_ref, wb_ref, h_ref, g_ref, b_ref, o_ref, ob_ref, *, alpha, tn):
    n = pl.program_id(1)
    cols = pl.ds(pl.multiple_of(n * tn, LANE), tn)
    o_ref[:, cols] = _dot(xa_ref[...], wa_ref[...]) + _dot(xb_ref[...], wb_ref[...])
    _res_ln_finish(n, h_ref, g_ref, b_ref, o_ref, ob_ref, alpha)


def _ffn_tail_kernel(ug, uv, lg, lv, fg, fv, cg, cv, bg, bv, w_ref, h_ref, g_ref, b_ref, o_ref, ob_ref,
                     act_a, act_b, *, alpha):
    k = pl.program_id(1)

    @pl.when(k == 0)
    def _():
        o_ref[...] = jnp.zeros_like(o_ref)
        act_a[...] = jnp.zeros_like(act_a)

    T, D = o_ref.shape
    tk = act_a.shape[1]
    pn = _pick(D, 512)
    n_pan = D // pn
    rb = _pick(T, 64, 16)
    i = pl.program_id(0)
    first_tile = i == 0
    last_tile = i == pl.num_programs(0) - 1
    row = lax.broadcasted_iota(jnp.int32, (rb, 1), 0)

    pk = 16

    def conv(u_ref, last_ref, first_ref, c_ref, cb_ref, r0, cols):
        u = u_ref[r0:r0 + rb, cols].astype(F32)
        if r0 == 0:
            up = jnp.where(first_tile, 0.0, last_ref[0, :, cols])
        else:
            up = u_ref[r0 - pk:r0, cols].astype(F32)[pk - 1:pk]
        if r0 + rb == T:
            un = jnp.where(last_tile, 0.0, first_ref[0, :, cols])
        else:
            un = u_ref[r0 + rb:r0 + rb + pk, cols].astype(F32)[0:1]
        prev = jnp.where(row == 0, up, pltpu.roll(u, 1, 0))
        nxt = jnp.where(row == rb - 1, un, pltpu.roll(u, rb - 1, 0))
        return (c_ref[0:1, cols] * prev + c_ref[1:2, cols] * u + c_ref[2:3, cols] * nxt
                + cb_ref[:, cols])

    blocks = [(r0, slice(c, c + LANE)) for c in range(0, tk, LANE) for r0 in range(0, T, rb)]

    def step(act_in, act_out):
        for q in range(n_pan):
            n = q * pn
            o_ref[:, n:n + pn] += _dot(act_in[...], w_ref[:, n:n + pn])
            for r0, cols in blocks[q * len(blocks) // n_pan:(q + 1) * len(blocks) // n_pan]:
                gate = conv(ug, lg, fg, cg, bg, r0, cols)
                val = conv(uv, lv, fv, cv, bv, r0, cols)
                gelu = 0.5 * gate * (1.0 + lax.erf(gate * (1.0 / math.sqrt(2.0))))
                act_out[r0:r0 + rb, cols] = (gelu * val).astype(BF16)

    @pl.when(k % 2 == 0)
    def _():
        step(act_a, act_b)

    @pl.when(k % 2 == 1)
    def _():
        step(act_b, act_a)

    _res_ln_finish(k, h_ref, g_ref, b_ref, o_ref, ob_ref, alpha)


def _row_tile_specs(T, D):
    h_spec = pl.BlockSpec((T, D), lambda i, k: (i, 0), pipeline_mode=pl.Buffered(1))
    vec = pl.BlockSpec((8, D), lambda i, k: (0, 0))
    out = pl.BlockSpec((T, D), lambda i, k: (i, 0))
    return h_spec, vec, out


def _proj_ln(xa, xb, w, h, g, b, alpha, T=512, tn=512, name="proj_ln"):
    M, Ka = xa.shape
    Kb = xb.shape[1]
    D = w.shape[1]
    assert Ka == Kb and w.shape[0] == Ka + Kb
    T = _pick(M, T, 8)
    tn = _pick(D, tn)
    h_spec, vec, out = _row_tile_specs(T, D)

    def x_spec(K):
        return pl.BlockSpec((T, K), lambda i, n: (i, 0), pipeline_mode=pl.Buffered(1))

    return pl.pallas_call(
        functools.partial(_proj_ln_kernel, alpha=alpha, tn=tn),
        grid=(M // T, D // tn),
        in_specs=[x_spec(Ka), x_spec(Kb),
                  pl.BlockSpec((Ka, tn), lambda i, n: (0, n)),
                  pl.BlockSpec((Kb, tn), lambda i, n: (1, n)), h_spec, vec, vec],
        out_specs=(out, out),
        out_shape=(jax.ShapeDtypeStruct((M, D), F32), jax.ShapeDtypeStruct((M, D), BF16)),
        compiler_params=_params(("parallel", "arbitrary")),
        name=name,
    )(xa, xb, w, w, h, jnp.broadcast_to(g, (8, D)), jnp.broadcast_to(b, (8, D)))


def _ffn_tail(u, u_first, u_last, conv_w, conv_b, w_down, h, g, b, alpha, T, tk=512):
    S, N2 = u.shape
    dffp = N2 // 2
    D = w_down.shape[1]
    tk = _pick(dffp, tk)
    nk = dffp // tk
    nb = S // T
    h_spec, vec, out = _row_tile_specs(T, D)

    def made(k):
        return jnp.minimum(k, nk - 1)

    def main(off):
        return pl.BlockSpec((T, tk), lambda i, k: (i, made(k) + off * nk))

    def cvec(off, rows):
        return pl.BlockSpec((rows, tk), lambda i, k: (0, made(k) + off * nk))

    hg = _halo_specs(nb, tk, made)
    hv = _halo_specs(nb, tk, lambda k: made(k) + nk)
    return pl.pallas_call(
        functools.partial(_ffn_tail_kernel, alpha=alpha),
        grid=(nb, nk + 1),
        in_specs=[main(0), main(1), hg[0], hv[0], hg[1], hv[1],
                  cvec(0, 3), cvec(1, 3), cvec(0, 1), cvec(1, 1),
                  pl.BlockSpec((tk, D), lambda i, k: (jnp.maximum(k - 1, 0), 0)), h_spec, vec, vec],
        out_specs=(out, out),
        out_shape=(jax.ShapeDtypeStruct((S, D), F32), jax.ShapeDtypeStruct((S, D), BF16)),
        scratch_shapes=[pltpu.VMEM((T, tk), BF16), pltpu.VMEM((T, tk), BF16)],
        compiler_params=_params(("parallel", "arbitrary")),
        name="ffn_tail",
    )(u, u, u_last, u_last, u_first, u_first, conv_w, conv_w, conv_b, conv_b, w_down, h,
      jnp.broadcast_to(g, (8, D)), jnp.broadcast_to(b, (8, D)))


def _layer_weights(l, W, gl, dl, al, w_in, mu_shift, decay_base, w_decay_up, aaa_base, w_aaa_up,
                   g_lora_up, k_k, k_a, r_k, lnx_g, lnx_b):
    gp = _round_up(gl, LANE)
    lp = _round_up(max(dl, al), LANE)
    o = 3 * W

    def padc(a, n):
        return jnp.pad(a, ((0, 0), (0, n - a.shape[1])))

    def padr(a, n):
        return jnp.pad(a, ((0, 0), (0, n - a.shape[1]), (0, 0)))

    wl, ml = w_in[l], mu_shift[l][None, :]
    segs = [(o, gl, gp), (o + gl, dl, lp), (o + gl + dl, dl, lp),
            (o + gl + 2 * dl, al, lp), (o + gl + 2 * dl + al, al, lp)]
    w_lora = jnp.concatenate([padc(wl[:, s:s + n], p) for s, n, p in segs], axis=1)
    mu_lora = jnp.concatenate([padc(ml[:, s:s + n], p) for s, n, p in segs], axis=1)
    n_shift = o + gl + 2 * dl + 2 * al
    hs = HEAD_SIZE
    idx = np.arange(LANE)
    bd = jnp.asarray(np.tile((idx[:, None] // hs) == (idx[None, :] // hs), (3, 1)), BF16)
    return dict(
        gp=gp, lp=lp,
        w_rkv=wl[:, :o].astype(BF16), w_lora=w_lora.astype(BF16), w_f=wl[:, n_shift:].astype(BF16),
        mu_rkv=ml[:, :o], mu_lora=mu_lora,
        decay_base=decay_base[l], aaa_base=aaa_base[l],
        w_decay_up=padr(w_decay_up[l], lp).astype(BF16), w_aaa_up=padr(w_aaa_up[l], lp).astype(BF16),
        g_up=jnp.pad(g_lora_up[l], ((0, gp - gl), (0, 0))).astype(BF16),
        k_k=k_k[l][None, :], k_a=k_a[l][None, :], r_k=r_k[l].reshape(1, W),
        lnx_g=lnx_g[l][None, :], lnx_b=lnx_b[l][None, :], bd=bd)


def kernel(x, ln0_g, ln0_b, w_in, mu_shift, decay_base, w_decay_up, aaa_base, w_aaa_up, g_lora_up, k_k, k_a, r_k, lnx_g, lnx_b, w_out, ln1_g, ln1_b, w_ffn_up, ffn_conv_w, ffn_conv_b, w_ffn_down, ln2_g, ln2_b):
    B, S, D = x.shape
    depth = w_in.shape[0]
    W = k_k.shape[1]
    gl, dl, al = g_lora_up.shape[1], w_decay_up.shape[2], w_aaa_up.shape[2]
    n_shift = 3 * W + gl + 2 * dl + 2 * al
    Fw = w_in.shape[2] - n_shift
    dff = w_ffn_down.shape[1]
    alpha = float((2 * depth) ** 0.25)
    assert B == 1 and S % CHUNK == 0 and W % LANE == 0 and LANE % HEAD_SIZE == 0

    tab = _fourier_tables(S, Fw // N_FOURIER_GROUPS)
    Tp = _pick(S, 256, 8)
    cwp = _pick(W, 512)
    Tf = _pick(S, 512, 8)
    dffp = _round_up(dff, 1024)
    pad = dffp - dff

    def pad_gv(a):
        cfg = [(0, 0)] * (a.ndim - 1) + [(0, pad)]
        return jnp.concatenate([jnp.pad(a[..., :dff], cfg), jnp.pad(a[..., dff:], cfg)], axis=-1)

    h, hb = _layer_norm(x.reshape(S, D), ln0_g, ln0_b)
    for l in range(depth):
        lw = _layer_weights(l, W, gl, dl, al, w_in, mu_shift, decay_base, w_decay_up, aaa_base,
                            w_aaa_up, g_lora_up, k_k, k_a, r_k, lnx_g, lnx_b)
        zrkv = _matmul_shift(hb, lw["w_rkv"], lw["mu_rkv"], name="in_rkv")
        zlora = _matmul_shift(hb, lw["w_lora"], lw["mu_lora"], name="in_lora")
        pf = _matmul(hb, lw["w_f"], BF16, name="in_fourier")
        r, v, kk, wl, b, kd, g, bg = _rwkv_prep(zrkv, zlora, lw, Tp, cwp)
        y2 = _rwkv_scan(r, v, kk, wl, b, kd)
        y_rwkv = _rwkv_finish(y2, g, bg, lw, Tf, cwp)
        y_four = _fourier_mixer(pf, tab)
        h, hb = _proj_ln(y_rwkv, y_four, w_out[l].astype(BF16), h, ln1_g[l], ln1_b[l], alpha, T=Tf,
                         name="out_proj_ln")
        u, u_first, u_last = _matmul(hb, pad_gv(w_ffn_up[l]).astype(BF16), BF16, name="ffn_up", edge_rows=Tf)
        w_down = jnp.pad(w_ffn_down[l], ((0, pad), (0, 0))).astype(BF16)
        h, hb = _ffn_tail(u, u_first, u_last, pad_gv(ffn_conv_w[l]), pad_gv(ffn_conv_b[l][None, :]),
                          w_down, h, ln2_g[l], ln2_b[l], alpha, Tf)
    return h.reshape(B, S, D)
```

```python
import functools
import math

import numpy as np
import jax
import jax.numpy as jnp
from jax import lax
from jax.experimental import pallas as pl
from jax.experimental.pallas import tpu as pltpu

F32 = jnp.float32
BF16 = jnp.bfloat16

LANE = 128
HEAD_SIZE = 64
N_FOURIER_GROUPS = 8
LN_EPS = 1e-5
GN_EPS = 64e-5
CHUNK = 64
SCAN_UNROLL = 16
VMEM_LIMIT = 56 * 1024 * 1024


def _round_up(n, m):
    return (n + m - 1) // m * m


def _pick(dim, target, align=LANE):
    if dim <= target:
        return dim
    best = None
    for t in range(align, target + 1, align):
        if dim % t == 0:
            best = t
    assert best is not None, (dim, target, align)
    return best


def _params(sem):
    return pltpu.CompilerParams(dimension_semantics=sem, vmem_limit_bytes=VMEM_LIMIT)


def _mm_kernel(x_ref, w_ref, o_ref):
    o_ref[...] = jnp.dot(x_ref[...], w_ref[...], preferred_element_type=F32).astype(o_ref.dtype)


def _mm_edges_kernel(x_ref, w_ref, o_ref, f_ref, l_ref, *, tb):
    res = jnp.dot(x_ref[...], w_ref[...], preferred_element_type=F32)
    o_ref[...] = res.astype(o_ref.dtype)
    for q in range(res.shape[0] // tb):
        f_ref[q] = res[q * tb:q * tb + 1, :]
        l_ref[q] = res[(q + 1) * tb - 1:(q + 1) * tb, :]


def _matmul(x, w, l, out_dtype, tm=1024, tn=1024, name="mm", edge_rows=None):
    M, K = x.shape
    N = w.shape[2]
    tm = _pick(M, tm, 8)
    tn = _pick(N, tn)
    in_specs = [pl.BlockSpec((tm, K), lambda j, i: (i, 0)),
                pl.BlockSpec((None, K, tn), lambda j, i: (l, 0, j))]
    out = pl.BlockSpec((tm, tn), lambda j, i: (i, j))
    if edge_rows is not None:
        assert tm % edge_rows == 0
        nq = tm // edge_rows
        edge = pl.BlockSpec((nq, 1, tn), lambda j, i: (i, 0, j))
        esd = jax.ShapeDtypeStruct((M // edge_rows, 1, N), F32)
        return pl.pallas_call(
            functools.partial(_mm_edges_kernel, tb=edge_rows),
            grid=(N // tn, M // tm),
            in_specs=in_specs,
            out_specs=(out, edge, edge),
            out_shape=(jax.ShapeDtypeStruct((M, N), out_dtype), esd, esd),
            compiler_params=_params(("parallel", "parallel")),
            name=name,
        )(x, w)
    return pl.pallas_call(
        _mm_kernel,
        grid=(N // tn, M // tm),
        in_specs=in_specs,
        out_specs=out,
        out_shape=jax.ShapeDtypeStruct((M, N), out_dtype),
        compiler_params=_params(("parallel", "parallel")),
        name=name,
    )(x, w)


BF16_ROWS = 16


def _mm_shift_kernel(x_ref, xp_ref, xn_ref, w_ref, mu_ref, o_ref, *, pn):
    i = pl.program_id(1)
    tm = x_ref.shape[0]
    xh = jnp.concatenate([xp_ref[...], xn_ref[...]], axis=0)
    def panel(n):
        cols = slice(n, n + pn)
        return _dot(x_ref[...], w_ref[:, cols]), _dot(xh, w_ref[:, cols])

    starts = list(range(0, o_ref.shape[1], pn))
    nxt_panel = panel(starts[0])
    for q, n in enumerate(starts):
        cols = slice(n, n + pn)
        p, ph = nxt_panel
        if q + 1 < len(starts):
            nxt_panel = panel(starts[q + 1])
        mu = mu_ref[:, cols]
        zp = jnp.where(i > 0, ph[BF16_ROWS - 1:BF16_ROWS], 0.0)
        zn = jnp.where(i < pl.num_programs(1) - 1, ph[BF16_ROWS:BF16_ROWS + 1], 0.0)
        o_ref[:, cols] = p + (0.5 * (pltpu.roll(p, 1, 0) + pltpu.roll(p, tm - 1, 0)) - p) * mu
        first, last = p[0:1], p[tm - 1:tm]
        o_ref[0:1, cols] = first + (0.5 * (zp + p[1:2]) - first) * mu
        o_ref[tm - 1:tm, cols] = last + (0.5 * (p[tm - 2:tm - 1] + zn) - last) * mu


def _matmul_shift(x, w, mu, l, tm=1024, tn=1024, name="mm_shift"):
    M, K = x.shape
    N = w.shape[2]
    tm = _pick(M, tm, BF16_ROWS)
    tn = _pick(N, tn)
    nh = tm // BF16_ROWS
    last = M // BF16_ROWS - 1
    return pl.pallas_call(
        functools.partial(_mm_shift_kernel, pn=_pick(tn, 256)),
        grid=(N // tn, M // tm),
        in_specs=[pl.BlockSpec((tm, K), lambda j, i: (i, 0)),
                  pl.BlockSpec((BF16_ROWS, K), lambda j, i: (jnp.maximum(i * nh - 1, 0), 0)),
                  pl.BlockSpec((BF16_ROWS, K), lambda j, i: (jnp.minimum((i + 1) * nh, last), 0)),
                  pl.BlockSpec((None, K, tn), lambda j, i: (l, 0, j)),
                  pl.BlockSpec((None, 1, tn), lambda j, i: (l, 0, j))],
        out_specs=pl.BlockSpec((tm, tn), lambda j, i: (i, j)),
        out_shape=jax.ShapeDtypeStruct((M, N), F32),
        compiler_params=_params(("parallel", "parallel")),
        name=name,
    )(x, x, x, w, mu)


def _ln_rows(x, g, b):
    mu = jnp.mean(x, axis=-1, keepdims=True)
    xc = x - mu
    var = jnp.mean(xc * xc, axis=-1, keepdims=True)
    return xc * lax.rsqrt(var + LN_EPS) * g + b


def _ln_kernel(x_ref, g_ref, b_ref, o_ref, ob_ref):
    y = _ln_rows(x_ref[...], g_ref[...], b_ref[...])
    o_ref[...] = y
    ob_ref[...] = y.astype(BF16)


def _layer_norm(x, g, b):
    M, D = x.shape
    tm = _pick(M, 256, 8)
    row = pl.BlockSpec((tm, D), lambda i: (i, 0))
    vec = pl.BlockSpec((1, D), lambda i: (0, 0))
    return pl.pallas_call(
        _ln_kernel, grid=(M // tm,), in_specs=[row, vec, vec], out_specs=(row, row),
        out_shape=(jax.ShapeDtypeStruct((M, D), F32), jax.ShapeDtypeStruct((M, D), BF16)),
        compiler_params=_params(("parallel",)), name="ln",
    )(x, g.reshape(1, D), b.reshape(1, D))


def _halo_specs(T_blocks, cw, col):
    prev = pl.BlockSpec((1, 1, cw), lambda i, j: (jnp.maximum(i - 1, 0), 0, col(j)))
    nxt = pl.BlockSpec((1, 1, cw), lambda i, j: (jnp.minimum(i + 1, T_blocks - 1), 0, col(j)))
    return prev, nxt


def _split3(w):
    w1 = w.astype(BF16)
    res = w - w1.astype(F32)
    w2 = res.astype(BF16)
    w3 = (res - w2.astype(F32)).astype(BF16)
    return jnp.concatenate([w1, w2, w3], axis=1)


def _seg_sum(x, bd3):
    parts = [jnp.dot(_split3(x[:, i * LANE:(i + 1) * LANE]), bd3, preferred_element_type=F32)
             for i in range(x.shape[1] // LANE)]
    return parts[0] if len(parts) == 1 else jnp.concatenate(parts, axis=1)


def _sigmoid(x):
    return 1.0 / (1.0 + jnp.exp(-x))


def _prep_kernel(zr, zk, zv, zlo, dbase, wdu, abase, wau, gup, kkw, kaw, rkw, bd,
                 r_o, v_o, kk_o, wl_o, b_o, kd_o, g_o, bg_o, *, gp, lp):
    r = zr[...]
    k = zk[...]
    v = zv[...]
    zl = zlo[...]
    bdm = bd[...]

    g = jnp.dot(_sigmoid(zl[:, :gp]).astype(BF16), gup[...], preferred_element_type=F32)
    kk = k * kkw[...]
    kk = kk / jnp.maximum(jnp.sqrt(_seg_sum(kk * kk, bdm)), 1e-12)
    ka = kaw[...]
    ksum = jnp.zeros_like(k)
    for d in range(2):
        wd = jnp.tanh(zl[:, gp + d * lp:gp + (d + 1) * lp]).astype(BF16)
        x = dbase[d:d + 1, :] + jnp.dot(wd, wdu[d], preferred_element_type=F32)
        wl_o[d] = -math.exp(-0.5) * _sigmoid(x)
        ad = zl[:, gp + (2 + d) * lp:gp + (3 + d) * lp].astype(BF16)
        a = _sigmoid(abase[d:d + 1, :] + jnp.dot(ad, wau[d], preferred_element_type=F32))
        kd = k * (1.0 + (a - 1.0) * ka)
        kd_o[d] = kd.astype(kd_o.dtype)
        b_o[d] = (a * kk).astype(b_o.dtype)
        ksum = ksum + kd
    bonus = _seg_sum(r * ksum * rkw[...], bdm) * v
    r_o[...] = r.astype(r_o.dtype)
    v_o[...] = v.astype(v_o.dtype)
    kk_o[...] = kk.astype(kk_o.dtype)
    g_o[...] = g.astype(g_o.dtype)
    bg_o[...] = (bonus * g).astype(bg_o.dtype)


def _rwkv_prep(zrkv, zlora, lw, T, cw):
    S = zrkv.shape[0]
    W = zrkv.shape[1] // 3
    LW = zlora.shape[1]
    gp, lp = lw["gp"], lw["lp"]
    nb, ncb = S // T, W // cw

    def main(off):
        return pl.BlockSpec((T, cw), lambda i, j: (i, j + off * ncb))

    def vec(rows=1):
        return pl.BlockSpec((rows, cw), lambda i, j: (0, j))

    in_specs = [main(0), main(1), main(2),
                pl.BlockSpec((T, LW), lambda i, j: (i, 0)),
                vec(2),
                pl.BlockSpec((2, lp, cw), lambda i, j: (0, 0, j)),
                vec(2),
                pl.BlockSpec((2, lp, cw), lambda i, j: (0, 0, j)),
                pl.BlockSpec((gp, cw), lambda i, j: (0, j)),
                vec(), vec(), vec(),
                pl.BlockSpec((3 * LANE, LANE), lambda i, j: (0, 0))]
    one = pl.BlockSpec((T, cw), lambda i, j: (i, j))
    two = pl.BlockSpec((2, T, cw), lambda i, j: (0, i, j))
    sd1 = jax.ShapeDtypeStruct((S, W), BF16)
    sd2 = jax.ShapeDtypeStruct((2, S, W), BF16)
    return pl.pallas_call(
        functools.partial(_prep_kernel, gp=gp, lp=lp),
        grid=(nb, ncb),
        in_specs=in_specs,
        out_specs=(one, one, one, two, two, two, one, one),
        out_shape=(sd1, sd1, sd1, jax.ShapeDtypeStruct((2, S, W), F32), sd2, sd2, sd1, sd1),
        compiler_params=_params(("parallel", "parallel")),
        name="rwkv_prep",
    )(zrkv, zrkv, zrkv, zlora, lw["decay_base"], lw["w_decay_up"],
      lw["aaa_base"], lw["w_aaa_up"], lw["g_up"], lw["k_k"], lw["k_a"], lw["r_k"], lw["bd"])


def _dot_nt(a, b):
    return lax.dot_general(a, b, (((1,), (1,)), ((), ())), preferred_element_type=F32)


def _dot_tn(a, b):
    return lax.dot_general(a, b, (((0,), (0,)), ((), ())), preferred_element_type=F32)


def _dot(a, b):
    return jnp.dot(a, b, preferred_element_type=F32)


def _scan_kernel(r_ref, v_ref, kk_ref, wl_ref, b_ref, kd_ref, y_ref,
                 state, *, C, n_slabs, hs, unroll):
    d = pl.program_id(0)
    c = pl.program_id(1)

    @pl.when(c == 0)
    def _():
        state[...] = jnp.zeros_like(state)

    sgn = 1 - 2 * d
    ri = lax.broadcasted_iota(jnp.int32, (C, C), 0)
    ci = lax.broadcasted_iota(jnp.int32, (C, C), 1)
    tri = ((ri - ci) * sgn >= 0).astype(BF16)

    n_heads = LANE // hs
    R = n_heads * C
    lc = int(math.log2(C))
    lh = int(math.log2(hs))
    n_sq = lc
    ri = lax.broadcasted_iota(jnp.int32, (C, 2 * R), 0)
    ci = lax.broadcasted_iota(jnp.int32, (C, 2 * R), 1)
    diff = (ri - (ci & (C - 1))) * sgn
    strict = diff > 0
    incl = diff >= 0
    sr = lax.broadcasted_iota(jnp.int32, (LANE, LANE), 0)
    sc = lax.broadcasted_iota(jnp.int32, (LANE, LANE), 1)
    same_head = (sr >> lh) == (sc >> lh)

    def stack(x, shift):
        grp = lax.broadcasted_iota(jnp.int32, x.shape, 1) >> shift
        return jnp.concatenate(
            [jnp.where(grp == h, x, jnp.zeros_like(x)) for h in range(n_heads)], axis=0)

    def lanes(p):
        return pl.ds(pl.multiple_of(p * LANE, LANE), LANE)

    def body(q, carry):
        ps = [q * unroll + j for j in range(unroll)]
        n = range(unroll)
        wl = [wl_ref[0, :, lanes(p)] for p in ps]
        s0 = [state[p] for p in ps]
        v = [v_ref[:, lanes(p)].astype(BF16) for p in ps]
        cum3 = [_dot(tri, _split3(wl[j])) for j in n]
        cum = [cum3[j][:, :LANE] + cum3[j][:, LANE:2 * LANE] + cum3[j][:, 2 * LANE:] for j in n]
        tot = [jnp.sum(wl[j], axis=0, keepdims=True) for j in n]
        m = [0.5 * tot[j] for j in n]
        e_iv = [jnp.exp(m[j] - cum[j]) for j in n]
        at = [(-kk_ref[:, lanes(ps[j])] * jnp.exp(cum[j] - wl[j] - m[j])).astype(BF16) for j in n]
        rt = [(r_ref[:, lanes(ps[j])] * jnp.exp(cum[j] - m[j])).astype(BF16) for j in n]
        bt = [(b_ref[0, :, lanes(ps[j])] * e_iv[j]).astype(BF16) for j in n]
        kt = [(kd_ref[0, :, lanes(ps[j])] * e_iv[j]).astype(BF16) for j in n]
        em = [jnp.exp(m[j]) for j in n]
        et = [jnp.exp(tot[j]) for j in n]
        s0s = [(s0[j] * em[j]).astype(BF16) for j in n]
        ar = [jnp.concatenate([at[j], rt[j]], axis=0) for j in n]
        big = [_dot_nt(ar[j], jnp.concatenate([stack(bt[j], lh), stack(kt[j], lh)], axis=0))
               for j in n]
        sx = [_dot_nt(ar[j], s0s[j]) for j in n]
        top = [jnp.where(strict, big[j][:C], jnp.zeros((C, 2 * R), F32)) for j in n]
        lab = [top[j][:, :R].astype(BF16) for j in n]
        lak = [top[j][:, R:].astype(BF16) for j in n]
        arbk = [jnp.where(incl, big[j][C:], jnp.zeros((C, 2 * R), F32)).astype(BF16) for j in n]
        v_s = [stack(v[j], lh) for j in n]
        x = [sx[j][:C] + _dot(lak[j], v_s[j]) for j in n]
        for i in range(n_sq - 1):
            both = [_dot(lab[j], jnp.concatenate([stack(x[j].astype(BF16), lh), stack(lab[j], lc)], axis=1))
                    for j in n]
            x = [x[j] + both[j][:, :LANE] for j in n]
            lab = [both[j][:, LANE:].astype(BF16) for j in n]
        u = [(x[j] + _dot(lab[j], stack(x[j].astype(BF16), lh))).astype(BF16) for j in n]
        y = [sx[j][C:] + _dot(arbk[j], jnp.concatenate([stack(u[j], lh), v_s[j]], axis=0)) for j in n]
        upd = [_dot_tn(jnp.concatenate([u[j], v[j]], axis=0), jnp.concatenate([bt[j], kt[j]], axis=0))
               for j in n]
        for j in n:
            y_ref[0, :, lanes(ps[j])] = y[j]
            state[ps[j]] = s0[j] * et[j] + jnp.where(same_head, upd[j], jnp.zeros_like(upd[j])) * em[j]
        return carry

    lax.fori_loop(0, n_slabs // unroll, body, 0)


def _rwkv_scan(r, v, kk, wl, b, kd):
    S, W = r.shape
    C = CHUNK
    nc = S // C
    n_slabs = W // LANE

    def shared(d, c):
        return (c + d * (nc - 1 - 2 * c), 0)

    def perdir(d, c):
        return (d, c + d * (nc - 1 - 2 * c), 0)

    s1 = pl.BlockSpec((C, W), shared)
    s2 = pl.BlockSpec((1, C, W), perdir)
    return pl.pallas_call(
        functools.partial(_scan_kernel, C=C, n_slabs=n_slabs, hs=HEAD_SIZE,
                          unroll=math.gcd(n_slabs, SCAN_UNROLL)),
        grid=(2, nc),
        in_specs=[s1, s1, s1, s2, s2, s2],
        out_specs=s2,
        out_shape=jax.ShapeDtypeStruct((2, S, W), F32),
        scratch_shapes=[pltpu.VMEM((n_slabs, LANE, LANE), F32)],
        compiler_params=_params(("arbitrary", "arbitrary")),
        name="rwkv_scan",
    )(r, v, kk, wl, b, kd)


def _fin_kernel(y_ref, g_ref, bg_ref, lg_ref, lb_ref, bd_ref, o_ref, *, hs):
    y = y_ref[0] + y_ref[1]
    bd = bd_ref[...]
    mean = _seg_sum(y, bd) * (1.0 / hs)
    yc = y - mean
    var = _seg_sum(yc * yc, bd) * (1.0 / hs)
    yn = yc * lax.rsqrt(var + GN_EPS) * lg_ref[...] + lb_ref[...]
    o_ref[...] = (yn * g_ref[...] + bg_ref[...]).astype(o_ref.dtype)


def _rwkv_finish(y2, g, bg, lw, T, cw):
    _, S, W = y2.shape
    one = pl.BlockSpec((T, cw), lambda i, j: (i, j))
    vec = pl.BlockSpec((1, cw), lambda i, j: (0, j))
    return pl.pallas_call(
        functools.partial(_fin_kernel, hs=HEAD_SIZE),
        grid=(S // T, W // cw),
        in_specs=[pl.BlockSpec((2, T, cw), lambda i, j: (0, i, j)), one, one, vec, vec,
                  pl.BlockSpec((3 * LANE, LANE), lambda i, j: (0, 0))],
        out_specs=one,
        out_shape=jax.ShapeDtypeStruct((S, W), BF16),
        compiler_params=_params(("parallel", "parallel")),
        name="rwkv_finish",
    )(y2, g, bg, lw["lnx_g"], lw["lnx_b"], lw["bd"])


def _dft1_kernel(x_ref, cs_ref, tc_ref, ts_ref, o_ref, *, n1):
    g = _dot(cs_ref[...], x_ref[...])
    gr, gi = g[:n1], g[n1:]
    reps = x_ref.shape[1] // LANE
    tc = jnp.concatenate([tc_ref[0]] * reps, axis=1)
    ts = jnp.concatenate([ts_ref[0]] * reps, axis=1)
    o_ref[0] = (gr * tc + gi * ts).astype(o_ref.dtype)
    o_ref[1] = (gi * tc - gr * ts).astype(o_ref.dtype)


def _dft2_kernel(g_ref, m2_ref, cc_ref, sc_ref, o_ref, *, n2, gw):
    fw = g_ref.shape[-1]
    g = g_ref[...].reshape(2 * n2, fw)
    ab = _dot(m2_ref[...], g)
    a = ab[:n2].astype(BF16)
    b = ab[n2:].astype(BF16)
    cc = cc_ref[...]
    sc = sc_ref[...]
    for q in range(fw // gw):
        sl = slice(q * gw, (q + 1) * gw)
        o_ref[:, sl] = (_dot(a[:, sl], cc) + _dot(b[:, sl], sc)).astype(o_ref.dtype)


def _fourier_tables(S, gw):
    n1 = 1 << (int(math.log2(S)) // 2)
    n2 = S // n1
    assert n1 * n2 == S
    i1 = np.arange(n1)
    ang1 = 2.0 * np.pi * np.outer(i1, i1) / n1
    cs1 = np.concatenate([np.cos(ang1), -np.sin(ang1)], axis=0)
    i2 = np.arange(n2)
    angt = 2.0 * np.pi * np.outer(i2, i1) / S
    ang2 = 2.0 * np.pi * np.outer(i2, i2) / n2
    c2, s2 = np.cos(ang2), np.sin(ang2)
    m2 = np.block([[c2, s2], [-s2, c2]])
    ic = np.arange(gw)
    angc = 2.0 * np.pi * np.outer(ic, ic) / gw
    scale = 1.0 / math.sqrt(S * gw)
    return dict(n1=n1, n2=n2,
                cs1=jnp.asarray(cs1, BF16),
                tc=jnp.asarray(np.cos(angt), F32), ts=jnp.asarray(np.sin(angt), F32),
                m2=jnp.asarray(m2, BF16),
                cc=jnp.asarray(np.cos(angc) * scale, BF16), sc=jnp.asarray(np.sin(angc) * scale, BF16))


def _fourier_mixer(pf, tab):
    S, Fw = pf.shape
    n1, n2 = tab["n1"], tab["n2"]
    gw = Fw // N_FOURIER_GROUPS
    tc = jnp.broadcast_to(tab["tc"][:, :, None], (n2, n1, LANE))
    ts = jnp.broadcast_to(tab["ts"][:, :, None], (n2, n1, LANE))
    x2 = pf.reshape(n1, n2 * Fw)
    g = pl.pallas_call(
        functools.partial(_dft1_kernel, n1=n1),
        grid=(n2,),
        in_specs=[pl.BlockSpec((n1, Fw), lambda j: (0, j)),
                  pl.BlockSpec((2 * n1, n1), lambda j: (0, 0)),
                  pl.BlockSpec((1, n1, LANE), lambda j: (j, 0, 0)),
                  pl.BlockSpec((1, n1, LANE), lambda j: (j, 0, 0))],
        out_specs=pl.BlockSpec((2, n1, Fw), lambda j: (0, 0, j)),
        out_shape=jax.ShapeDtypeStruct((2, n1, n2 * Fw), BF16),
        compiler_params=_params(("parallel",)),
        name="dft_stage1",
    )(x2, tab["cs1"], tc, ts)
    g4 = g.reshape(2, n1, n2, Fw)
    out = pl.pallas_call(
        functools.partial(_dft2_kernel, n2=n2, gw=gw),
        grid=(n1,),
        in_specs=[pl.BlockSpec((2, None, n2, Fw), lambda i: (0, i, 0, 0)),
                  pl.BlockSpec((2 * n2, 2 * n2), lambda i: (0, 0)),
                  pl.BlockSpec((gw, gw), lambda i: (0, 0)),
                  pl.BlockSpec((gw, gw), lambda i: (0, 0))],
        out_specs=pl.BlockSpec((n2, Fw), lambda i: (0, i)),
        out_shape=jax.ShapeDtypeStruct((n2, n1 * Fw), BF16),
        compiler_params=_params(("parallel",)),
        name="dft_stage2",
    )(g4, tab["m2"], tab["cc"], tab["sc"])
    return out.reshape(S, Fw)


def _acc_dot(o_ref, x, w_ref):
    D = o_ref.shape[1]
    pn = _pick(D, 512)
    for n in range(0, D, pn):
        o_ref[:, n:n + pn] += _dot(x, w_ref[:, n:n + pn])


def _res_ln_finish(k, h_ref, g_ref, b_ref, o_ref, ob_ref, alpha):
    @pl.when(k == pl.num_programs(1) - 1)
    def _():
        T = o_ref.shape[0]
        for r in range(0, T, 16):
            for r8 in (r, r + 8):
                rows = pl.ds(r8, 8)
                o_ref[rows, :] = _ln_rows(alpha * h_ref[rows, :] + o_ref[rows, :], g_ref[...], b_ref[...])
            ob_ref[pl.ds(r, 16), :] = o_ref[pl.ds(r, 16), :].astype(BF16)


def _proj_ln_kernel(xa_ref, xb_ref, wa_ref, wb_ref, h_ref, g_ref, b_ref, o_ref, ob_ref, *, alpha, tn):
    n = pl.program_id(1)
    cols = pl.ds(pl.multiple_of(n * tn, LANE), tn)
    o_ref[:, cols] = _dot(xa_ref[...], wa_ref[...]) + _dot(xb_ref[...], wb_ref[...])
    _res_ln_finish(n, h_ref, g_ref, b_ref, o_ref, ob_ref, alpha)


def _ffn_tail_kernel(ug, uv, lg, lv, fg, fv, cg, cv, bg, bv, w_ref, h_ref, g_ref, b_ref, o_ref, ob_ref,
                     act_a, act_b, *, alpha):
    k = pl.program_id(1)

    @pl.when(k == 0)
    def _():
        o_ref[...] = jnp.zeros_like(o_ref)
        act_a[...] = jnp.zeros_like(act_a)

    T, D = o_ref.shape
    tk = act_a.shape[1]
    pn = _pick(D, 512)
    n_pan = D // pn
    rb = _pick(T, 64, 16)
    i = pl.program_id(0)
    first_tile = i == 0
    last_tile = i == pl.num_programs(0) - 1
    row = lax.broadcasted_iota(jnp.int32, (rb, 1), 0)

    pk = 16

    def conv(u_ref, last_ref, first_ref, c_ref, cb_ref, r0, cols):
        u = u_ref[r0:r0 + rb, cols].astype(F32)
        if r0 == 0:
            up = jnp.where(first_tile, 0.0, last_ref[0, :, cols])
        else:
            up = u_ref[r0 - pk:r0, cols].astype(F32)[pk - 1:pk]
        if r0 + rb == T:
            un = jnp.where(last_tile, 0.0, first_ref[0, :, cols])
        else:
            un = u_ref[r0 + rb:r0 + rb + pk, cols].astype(F32)[0:1]
        prev = jnp.where(row == 0, up, pltpu.roll(u, 1, 0))
        nxt = jnp.where(row == rb - 1, un, pltpu.roll(u, rb - 1, 0))
        return (c_ref[0:1, cols] * prev + c_ref[1:2, cols] * u + c_ref[2:3, cols] * nxt
                + cb_ref[:, cols])

    blocks = [(r0, slice(c, c + LANE)) for c in range(0, tk, LANE) for r0 in range(0, T, rb)]

    def step(act_in, act_out):
        for q in range(n_pan):
            n = q * pn
            o_ref[:, n:n + pn] += _dot(act_in[...], w_ref[:, n:n + pn])
            for r0, cols in blocks[q * len(blocks) // n_pan:(q + 1) * len(blocks) // n_pan]:
                gate = conv(ug, lg, fg, cg, bg, r0, cols)
                val = conv(uv, lv, fv, cv, bv, r0, cols)
                gelu = 0.5 * gate * (1.0 + lax.erf(gate * (1.0 / math.sqrt(2.0))))
                act_out[r0:r0 + rb, cols] = (gelu * val).astype(BF16)

    @pl.when(k % 2 == 0)
    def _():
        step(act_a, act_b)

    @pl.when(k % 2 == 1)
    def _():
        step(act_b, act_a)

    _res_ln_finish(k, h_ref, g_ref, b_ref, o_ref, ob_ref, alpha)


def _row_tile_specs(T, D):
    h_spec = pl.BlockSpec((T, D), lambda i, k: (i, 0), pipeline_mode=pl.Buffered(1))
    vec = pl.BlockSpec((8, D), lambda i, k: (0, 0))
    out = pl.BlockSpec((T, D), lambda i, k: (i, 0))
    return h_spec, vec, out


def _proj_ln(xa, xb, w, l, h, g, b, alpha, T=512, tn=512, name="proj_ln"):
    M, Ka = xa.shape
    Kb = xb.shape[1]
    D = w.shape[2]
    assert Ka == Kb and w.shape[1] == Ka + Kb
    T = _pick(M, T, 8)
    tn = _pick(D, tn)
    h_spec, vec, out = _row_tile_specs(T, D)

    def x_spec(K):
        return pl.BlockSpec((T, K), lambda i, n: (i, 0), pipeline_mode=pl.Buffered(1))

    return pl.pallas_call(
        functools.partial(_proj_ln_kernel, alpha=alpha, tn=tn),
        grid=(M // T, D // tn),
        in_specs=[x_spec(Ka), x_spec(Kb),
                  pl.BlockSpec((None, Ka, tn), lambda i, n: (l, 0, n)),
                  pl.BlockSpec((None, Kb, tn), lambda i, n: (l, 1, n)), h_spec, vec, vec],
        out_specs=(out, out),
        out_shape=(jax.ShapeDtypeStruct((M, D), F32), jax.ShapeDtypeStruct((M, D), BF16)),
        compiler_params=_params(("parallel", "arbitrary")),
        name=name,
    )(xa, xb, w, w, h, jnp.broadcast_to(g, (8, D)), jnp.broadcast_to(b, (8, D)))


def _ffn_tail(u, u_first, u_last, conv_w, conv_b, w_down, l, h, g, b, alpha, T, tk=512):
    S, N2 = u.shape
    dffp = N2 // 2
    D = w_down.shape[2]
    tk = _pick(dffp, tk)
    nk = dffp // tk
    nb = S // T
    h_spec, vec, out = _row_tile_specs(T, D)

    def made(k):
        return jnp.minimum(k, nk - 1)

    def main(off):
        return pl.BlockSpec((T, tk), lambda i, k: (i, made(k) + off * nk))

    def cvec(off, rows):
        return pl.BlockSpec((rows, tk), lambda i, k: (0, made(k) + off * nk))

    hg = _halo_specs(nb, tk, made)
    hv = _halo_specs(nb, tk, lambda k: made(k) + nk)
    return pl.pallas_call(
        functools.partial(_ffn_tail_kernel, alpha=alpha),
        grid=(nb, nk + 1),
        in_specs=[main(0), main(1), hg[0], hv[0], hg[1], hv[1],
                  cvec(0, 3), cvec(1, 3), cvec(0, 1), cvec(1, 1),
                  pl.BlockSpec((None, tk, D), lambda i, k: (l, jnp.maximum(k - 1, 0), 0)), h_spec, vec, vec],
        out_specs=(out, out),
        out_shape=(jax.ShapeDtypeStruct((S, D), F32), jax.ShapeDtypeStruct((S, D), BF16)),
        scratch_shapes=[pltpu.VMEM((T, tk), BF16), pltpu.VMEM((T, tk), BF16)],
        compiler_params=_params(("parallel", "arbitrary")),
        name="ffn_tail",
    )(u, u, u_last, u_last, u_first, u_first, conv_w, conv_w, conv_b, conv_b, w_down, h,
      jnp.broadcast_to(g, (8, D)), jnp.broadcast_to(b, (8, D)))


def _stacked_weights(W, gl, dl, al, w_in, mu_shift):
    gp = _round_up(gl, LANE)
    lp = _round_up(max(dl, al), LANE)
    o = 3 * W

    def padc(a, n):
        return jnp.pad(a, [(0, 0)] * (a.ndim - 1) + [(0, n - a.shape[-1])])

    mu = mu_shift[:, None, :]
    segs = [(o, gl, gp), (o + gl, dl, lp), (o + gl + dl, dl, lp),
            (o + gl + 2 * dl, al, lp), (o + gl + 2 * dl + al, al, lp)]
    n_shift = o + gl + 2 * dl + 2 * al
    return dict(
        gp=gp, lp=lp,
        w_rkv=w_in[:, :, :o].astype(BF16),
        w_lora=jnp.concatenate([padc(w_in[:, :, s:s + n], p) for s, n, p in segs], axis=-1).astype(BF16),
        w_f=w_in[:, :, n_shift:].astype(BF16),
        mu_rkv=mu[:, :, :o],
        mu_lora=jnp.concatenate([padc(mu[:, :, s:s + n], p) for s, n, p in segs], axis=-1))


def _layer_weights(l, sw, W, decay_base, w_decay_up, aaa_base, w_aaa_up, g_lora_up, k_k, k_a, r_k,
                   lnx_g, lnx_b):
    gp, lp = sw["gp"], sw["lp"]

    def padr(a, n):
        return jnp.pad(a, ((0, 0), (0, n - a.shape[1]), (0, 0)))

    hs = HEAD_SIZE
    idx = np.arange(LANE)
    bd = jnp.asarray(np.tile((idx[:, None] // hs) == (idx[None, :] // hs), (3, 1)), BF16)
    return dict(
        gp=gp, lp=lp,
        decay_base=decay_base[l], aaa_base=aaa_base[l],
        w_decay_up=padr(w_decay_up[l], lp).astype(BF16), w_aaa_up=padr(w_aaa_up[l], lp).astype(BF16),
        g_up=jnp.pad(g_lora_up[l], ((0, gp - g_lora_up.shape[1]), (0, 0))).astype(BF16),
        k_k=k_k[l][None, :], k_a=k_a[l][None, :], r_k=r_k[l].reshape(1, W),
        lnx_g=lnx_g[l][None, :], lnx_b=lnx_b[l][None, :], bd=bd)


def kernel(x, ln0_g, ln0_b, w_in, mu_shift, decay_base, w_decay_up, aaa_base, w_aaa_up, g_lora_up, k_k, k_a, r_k, lnx_g, lnx_b, w_out, ln1_g, ln1_b, w_ffn_up, ffn_conv_w, ffn_conv_b, w_ffn_down, ln2_g, ln2_b):
    B, S, D = x.shape
    depth = w_in.shape[0]
    W = k_k.shape[1]
    gl, dl, al = g_lora_up.shape[1], w_decay_up.shape[2], w_aaa_up.shape[2]
    n_shift = 3 * W + gl + 2 * dl + 2 * al
    Fw = w_in.shape[2] - n_shift
    dff = w_ffn_down.shape[1]
    alpha = float((2 * depth) ** 0.25)
    assert B == 1 and S % CHUNK == 0 and W % LANE == 0 and LANE % HEAD_SIZE == 0

    tab = _fourier_tables(S, Fw // N_FOURIER_GROUPS)
    Tp = _pick(S, 256, 8)
    cwp = _pick(W, 512)
    Tf = _pick(S, 512, 8)
    dffp = _round_up(dff, 1024)
    pad = dffp - dff

    def pad_gv(a):
        cfg = [(0, 0)] * (a.ndim - 1) + [(0, pad)]
        return jnp.concatenate([jnp.pad(a[..., :dff], cfg), jnp.pad(a[..., dff:], cfg)], axis=-1)

    sw = _stacked_weights(W, gl, dl, al, w_in, mu_shift)
    w_out_b = w_out.astype(BF16)
    w_up_b = pad_gv(w_ffn_up).astype(BF16)
    w_down_b = jnp.pad(w_ffn_down, ((0, 0), (0, pad), (0, 0))).astype(BF16)
    conv_w, conv_b = pad_gv(ffn_conv_w), pad_gv(ffn_conv_b[:, None, :])

    h, hb = _layer_norm(x.reshape(S, D), ln0_g, ln0_b)
    for l in range(depth):
        lw = _layer_weights(l, sw, W, decay_base, w_decay_up, aaa_base, w_aaa_up, g_lora_up, k_k, k_a,
                            r_k, lnx_g, lnx_b)
        zrkv = _matmul_shift(hb, sw["w_rkv"], sw["mu_rkv"], l, name="in_rkv")
        zlora = _matmul_shift(hb, sw["w_lora"], sw["mu_lora"], l, name="in_lora")
        pf = _matmul(hb, sw["w_f"], l, BF16, name="in_fourier")
        r, v, kk, wl, b, kd, g, bg = _rwkv_prep(zrkv, zlora, lw, Tp, cwp)
        y2 = _rwkv_scan(r, v, kk, wl, b, kd)
        y_rwkv = _rwkv_finish(y2, g, bg, lw, Tf, cwp)
        y_four = _fourier_mixer(pf, tab)
        h, hb = _proj_ln(y_rwkv, y_four, w_out_b, l, h, ln1_g[l], ln1_b[l], alpha, T=Tf, name="out_proj_ln")
        u, u_first, u_last = _matmul(hb, w_up_b, l, BF16, name="ffn_up", edge_rows=Tf)
        h, hb = _ffn_tail(u, u_first, u_last, conv_w[l], conv_b[l], w_down_b, l, h, ln2_g[l], ln2_b[l],
                          alpha, Tf)
    return h.reshape(B, S, D)
```

```python
import functools
import math

import numpy as np
import jax
import jax.numpy as jnp
from jax import lax
from jax.experimental import pallas as pl
from jax.experimental.pallas import tpu as pltpu

F32 = jnp.float32
BF16 = jnp.bfloat16

LANE = 128
HEAD_SIZE = 64
N_FOURIER_GROUPS = 8
LN_EPS = 1e-5
GN_EPS = 64e-5
CHUNK = 64
SCAN_UNROLL = 16
VMEM_LIMIT = 56 * 1024 * 1024


def _round_up(n, m):
    return (n + m - 1) // m * m


def _pick(dim, target, align=LANE):
    if dim <= target:
        return dim
    best = None
    for t in range(align, target + 1, align):
        if dim % t == 0:
            best = t
    assert best is not None, (dim, target, align)
    return best


def _params(sem):
    return pltpu.CompilerParams(dimension_semantics=sem, vmem_limit_bytes=VMEM_LIMIT)


def _mm_kernel(x_ref, w_ref, o_ref):
    o_ref[...] = jnp.dot(x_ref[...], w_ref[...], preferred_element_type=F32).astype(o_ref.dtype)


def _mm_edges_kernel(x_ref, w_ref, o_ref, f_ref, l_ref, *, tb):
    res = jnp.dot(x_ref[...], w_ref[...], preferred_element_type=F32)
    o_ref[...] = res.astype(o_ref.dtype)
    for q in range(res.shape[0] // tb):
        f_ref[q] = res[q * tb:q * tb + 1, :]
        l_ref[q] = res[(q + 1) * tb - 1:(q + 1) * tb, :]


def _matmul(x, w, l, out_dtype, tm=1024, tn=1024, name="mm", edge_rows=None):
    M, K = x.shape
    N = w.shape[2]
    tm = _pick(M, tm, 8)
    tn = _pick(N, tn)
    in_specs = [pl.BlockSpec((tm, K), lambda j, i: (i, 0)),
                pl.BlockSpec((None, K, tn), lambda j, i: (l, 0, j))]
    out = pl.BlockSpec((tm, tn), lambda j, i: (i, j))
    if edge_rows is not None:
        assert tm % edge_rows == 0
        nq = tm // edge_rows
        edge = pl.BlockSpec((nq, 1, tn), lambda j, i: (i, 0, j))
        esd = jax.ShapeDtypeStruct((M // edge_rows, 1, N), F32)
        return pl.pallas_call(
            functools.partial(_mm_edges_kernel, tb=edge_rows),
            grid=(N // tn, M // tm),
            in_specs=in_specs,
            out_specs=(out, edge, edge),
            out_shape=(jax.ShapeDtypeStruct((M, N), out_dtype), esd, esd),
            compiler_params=_params(("parallel", "parallel")),
            name=name,
        )(x, w)
    return pl.pallas_call(
        _mm_kernel,
        grid=(N // tn, M // tm),
        in_specs=in_specs,
        out_specs=out,
        out_shape=jax.ShapeDtypeStruct((M, N), out_dtype),
        compiler_params=_params(("parallel", "parallel")),
        name=name,
    )(x, w)


BF16_ROWS = 16


def _mm_shift_kernel(x_ref, xp_ref, xn_ref, w_ref, mu_ref, o_ref, *, pn):
    i = pl.program_id(1)
    tm = x_ref.shape[0]
    xh = jnp.concatenate([xp_ref[...], xn_ref[...]], axis=0)
    def panel(n):
        cols = slice(n, n + pn)
        return _dot(x_ref[...], w_ref[:, cols]), _dot(xh, w_ref[:, cols])

    starts = list(range(0, o_ref.shape[1], pn))
    nxt_panel = panel(starts[0])
    for q, n in enumerate(starts):
        cols = slice(n, n + pn)
        p, ph = nxt_panel
        if q + 1 < len(starts):
            nxt_panel = panel(starts[q + 1])
        mu = mu_ref[:, cols]
        zp = jnp.where(i > 0, ph[BF16_ROWS - 1:BF16_ROWS], 0.0)
        zn = jnp.where(i < pl.num_programs(1) - 1, ph[BF16_ROWS:BF16_ROWS + 1], 0.0)
        o_ref[:, cols] = p + (0.5 * (pltpu.roll(p, 1, 0) + pltpu.roll(p, tm - 1, 0)) - p) * mu
        first, last = p[0:1], p[tm - 1:tm]
        o_ref[0:1, cols] = first + (0.5 * (zp + p[1:2]) - first) * mu
        o_ref[tm - 1:tm, cols] = last + (0.5 * (p[tm - 2:tm - 1] + zn) - last) * mu


def _matmul_shift(x, w, mu, l, tm=1024, tn=1024, name="mm_shift"):
    M, K = x.shape
    N = w.shape[2]
    tm = _pick(M, tm, BF16_ROWS)
    tn = _pick(N, tn)
    nh = tm // BF16_ROWS
    last = M // BF16_ROWS - 1
    return pl.pallas_call(
        functools.partial(_mm_shift_kernel, pn=_pick(tn, 256)),
        grid=(N // tn, M // tm),
        in_specs=[pl.BlockSpec((tm, K), lambda j, i: (i, 0)),
                  pl.BlockSpec((BF16_ROWS, K), lambda j, i: (jnp.maximum(i * nh - 1, 0), 0)),
                  pl.BlockSpec((BF16_ROWS, K), lambda j, i: (jnp.minimum((i + 1) * nh, last), 0)),
                  pl.BlockSpec((None, K, tn), lambda j, i: (l, 0, j)),
                  pl.BlockSpec((None, 1, tn), lambda j, i: (l, 0, j))],
        out_specs=pl.BlockSpec((tm, tn), lambda j, i: (i, j)),
        out_shape=jax.ShapeDtypeStruct((M, N), F32),
        compiler_params=_params(("parallel", "parallel")),
        name=name,
    )(x, x, x, w, mu)


def _ln_rows(x, g, b):
    mu = jnp.mean(x, axis=-1, keepdims=True)
    xc = x - mu
    var = jnp.mean(xc * xc, axis=-1, keepdims=True)
    return xc * lax.rsqrt(var + LN_EPS) * g + b


def _ln_kernel(x_ref, g_ref, b_ref, o_ref, ob_ref):
    y = _ln_rows(x_ref[...], g_ref[...], b_ref[...])
    o_ref[...] = y
    ob_ref[...] = y.astype(BF16)


def _layer_norm(x, g, b):
    M, D = x.shape
    tm = _pick(M, 256, 8)
    row = pl.BlockSpec((tm, D), lambda i: (i, 0))
    vec = pl.BlockSpec((1, D), lambda i: (0, 0))
    return pl.pallas_call(
        _ln_kernel, grid=(M // tm,), in_specs=[row, vec, vec], out_specs=(row, row),
        out_shape=(jax.ShapeDtypeStruct((M, D), F32), jax.ShapeDtypeStruct((M, D), BF16)),
        compiler_params=_params(("parallel",)), name="ln",
    )(x, g.reshape(1, D), b.reshape(1, D))


def _halo_specs(T_blocks, cw, col):
    prev = pl.BlockSpec((1, 1, cw), lambda i, j: (jnp.maximum(i - 1, 0), 0, col(j)))
    nxt = pl.BlockSpec((1, 1, cw), lambda i, j: (jnp.minimum(i + 1, T_blocks - 1), 0, col(j)))
    return prev, nxt


def _split3(w):
    w1 = w.astype(BF16)
    res = w - w1.astype(F32)
    w2 = res.astype(BF16)
    w3 = (res - w2.astype(F32)).astype(BF16)
    return jnp.concatenate([w1, w2, w3], axis=1)


def _seg_sum(x, bd3):
    parts = [jnp.dot(_split3(x[:, i * LANE:(i + 1) * LANE]), bd3, preferred_element_type=F32)
             for i in range(x.shape[1] // LANE)]
    return parts[0] if len(parts) == 1 else jnp.concatenate(parts, axis=1)


def _sigmoid(x):
    return 1.0 / (1.0 + jnp.exp(-x))


def _prep_kernel(zr, zk, zv, zlo, dbase, wdu, abase, wau, gup, kkw, kaw, rkw, bd,
                 r_o, v_o, kk_o, wl_o, b_o, kd_o, g_o, bg_o, *, gp, lp):
    r = zr[...]
    k = zk[...]
    v = zv[...]
    zl = zlo[...]
    bdm = bd[...]

    g = jnp.dot(_sigmoid(zl[:, :gp]).astype(BF16), gup[...], preferred_element_type=F32)
    kk = k * kkw[...]
    kk = kk / jnp.maximum(jnp.sqrt(_seg_sum(kk * kk, bdm)), 1e-12)
    ka = kaw[...]
    ksum = jnp.zeros_like(k)
    for d in range(2):
        wd = jnp.tanh(zl[:, gp + d * lp:gp + (d + 1) * lp]).astype(BF16)
        x = dbase[d:d + 1, :] + jnp.dot(wd, wdu[d], preferred_element_type=F32)
        wl_o[d] = -math.exp(-0.5) * _sigmoid(x)
        ad = zl[:, gp + (2 + d) * lp:gp + (3 + d) * lp].astype(BF16)
        a = _sigmoid(abase[d:d + 1, :] + jnp.dot(ad, wau[d], preferred_element_type=F32))
        kd = k * (1.0 + (a - 1.0) * ka)
        kd_o[d] = kd.astype(kd_o.dtype)
        b_o[d] = (a * kk).astype(b_o.dtype)
        ksum = ksum + kd
    bonus = _seg_sum(r * ksum * rkw[...], bdm) * v
    r_o[...] = r.astype(r_o.dtype)
    v_o[...] = v.astype(v_o.dtype)
    kk_o[...] = kk.astype(kk_o.dtype)
    g_o[...] = g.astype(g_o.dtype)
    bg_o[...] = (bonus * g).astype(bg_o.dtype)


def _rwkv_prep(zrkv, zlora, lw, T, cw):
    S = zrkv.shape[0]
    W = zrkv.shape[1] // 3
    LW = zlora.shape[1]
    gp, lp = lw["gp"], lw["lp"]
    nb, ncb = S // T, W // cw

    def main(off):
        return pl.BlockSpec((T, cw), lambda i, j: (i, j + off * ncb))

    def vec(rows=1):
        return pl.BlockSpec((rows, cw), lambda i, j: (0, j))

    in_specs = [main(0), main(1), main(2),
                pl.BlockSpec((T, LW), lambda i, j: (i, 0)),
                vec(2),
                pl.BlockSpec((2, lp, cw), lambda i, j: (0, 0, j)),
                vec(2),
                pl.BlockSpec((2, lp, cw), lambda i, j: (0, 0, j)),
                pl.BlockSpec((gp, cw), lambda i, j: (0, j)),
                vec(), vec(), vec(),
                pl.BlockSpec((3 * LANE, LANE), lambda i, j: (0, 0))]
    one = pl.BlockSpec((T, cw), lambda i, j: (i, j))
    two = pl.BlockSpec((2, T, cw), lambda i, j: (0, i, j))
    sd1 = jax.ShapeDtypeStruct((S, W), BF16)
    sd2 = jax.ShapeDtypeStruct((2, S, W), BF16)
    return pl.pallas_call(
        functools.partial(_prep_kernel, gp=gp, lp=lp),
        grid=(nb, ncb),
        in_specs=in_specs,
        out_specs=(one, one, one, two, two, two, one, one),
        out_shape=(sd1, sd1, sd1, jax.ShapeDtypeStruct((2, S, W), F32), sd2, sd2, sd1, sd1),
        compiler_params=_params(("parallel", "parallel")),
        name="rwkv_prep",
    )(zrkv, zrkv, zrkv, zlora, lw["decay_base"], lw["w_decay_up"],
      lw["aaa_base"], lw["w_aaa_up"], lw["g_up"], lw["k_k"], lw["k_a"], lw["r_k"], lw["bd"])


def _dot_nt(a, b):
    return lax.dot_general(a, b, (((1,), (1,)), ((), ())), preferred_element_type=F32)


def _dot_tn(a, b):
    return lax.dot_general(a, b, (((0,), (0,)), ((), ())), preferred_element_type=F32)


def _dot(a, b):
    return jnp.dot(a, b, preferred_element_type=F32)


def _scan_kernel(r_ref, v_ref, kk_ref, wl_ref, b_ref, kd_ref, y_ref,
                 state, *, C, n_slabs, hs, unroll):
    d = pl.program_id(0)
    c = pl.program_id(1)

    @pl.when(c == 0)
    def _():
        state[...] = jnp.zeros_like(state)

    sgn = 1 - 2 * d
    ri = lax.broadcasted_iota(jnp.int32, (C, C), 0)
    ci = lax.broadcasted_iota(jnp.int32, (C, C), 1)
    tri = ((ri - ci) * sgn >= 0).astype(BF16)

    n_heads = LANE // hs
    R = n_heads * C
    lc = int(math.log2(C))
    lh = int(math.log2(hs))
    n_sq = lc
    ri = lax.broadcasted_iota(jnp.int32, (C, 2 * R), 0)
    ci = lax.broadcasted_iota(jnp.int32, (C, 2 * R), 1)
    diff = (ri - (ci & (C - 1))) * sgn
    strict = diff > 0
    incl = diff >= 0
    sr = lax.broadcasted_iota(jnp.int32, (LANE, LANE), 0)
    sc = lax.broadcasted_iota(jnp.int32, (LANE, LANE), 1)
    same_head = (sr >> lh) == (sc >> lh)

    def stack(x, shift):
        grp = lax.broadcasted_iota(jnp.int32, x.shape, 1) >> shift
        return jnp.concatenate(
            [jnp.where(grp == h, x, jnp.zeros_like(x)) for h in range(n_heads)], axis=0)

    def lanes(p):
        return pl.ds(pl.multiple_of(p * LANE, LANE), LANE)

    def body(q, carry):
        ps = [q * unroll + j for j in range(unroll)]
        n = range(unroll)
        wl = [wl_ref[0, :, lanes(p)] for p in ps]
        s0 = [state[p] for p in ps]
        v = [v_ref[:, lanes(p)].astype(BF16) for p in ps]
        cum3 = [_dot(tri, _split3(wl[j])) for j in n]
        cum = [cum3[j][:, :LANE] + cum3[j][:, LANE:2 * LANE] + cum3[j][:, 2 * LANE:] for j in n]
        tot = [jnp.sum(wl[j], axis=0, keepdims=True) for j in n]
        m = [0.5 * tot[j] for j in n]
        e_iv = [jnp.exp(m[j] - cum[j]) for j in n]
        at = [(-kk_ref[:, lanes(ps[j])] * jnp.exp(cum[j] - wl[j] - m[j])).astype(BF16) for j in n]
        rt = [(r_ref[:, lanes(ps[j])] * jnp.exp(cum[j] - m[j])).astype(BF16) for j in n]
        bt = [(b_ref[0, :, lanes(ps[j])] * e_iv[j]).astype(BF16) for j in n]
        kt = [(kd_ref[0, :, lanes(ps[j])] * e_iv[j]).astype(BF16) for j in n]
        em = [jnp.exp(m[j]) for j in n]
        et = [jnp.exp(tot[j]) for j in n]
        s0s = [(s0[j] * em[j]).astype(BF16) for j in n]
        ar = [jnp.concatenate([at[j], rt[j]], axis=0) for j in n]
        big = [_dot_nt(ar[j], jnp.concatenate([stack(bt[j], lh), stack(kt[j], lh)], axis=0))
               for j in n]
        sx = [_dot_nt(ar[j], s0s[j]) for j in n]
        top = [jnp.where(strict, big[j][:C], jnp.zeros((C, 2 * R), F32)) for j in n]
        lab = [top[j][:, :R].astype(BF16) for j in n]
        lak = [top[j][:, R:].astype(BF16) for j in n]
        arbk = [jnp.where(incl, big[j][C:], jnp.zeros((C, 2 * R), F32)).astype(BF16) for j in n]
        v_s = [stack(v[j], lh) for j in n]
        x = [sx[j][:C] + _dot(lak[j], v_s[j]) for j in n]
        for i in range(n_sq - 1):
            both = [_dot(lab[j], jnp.concatenate([stack(x[j].astype(BF16), lh), stack(lab[j], lc)], axis=1))
                    for j in n]
            x = [x[j] + both[j][:, :LANE] for j in n]
            lab = [both[j][:, LANE:].astype(BF16) for j in n]
        u = [(x[j] + _dot(lab[j], stack(x[j].astype(BF16), lh))).astype(BF16) for j in n]
        y = [sx[j][C:] + _dot(arbk[j], jnp.concatenate([stack(u[j], lh), v_s[j]], axis=0)) for j in n]
        upd = [_dot_tn(jnp.concatenate([u[j], v[j]], axis=0), jnp.concatenate([bt[j], kt[j]], axis=0))
               for j in n]
        for j in n:
            y_ref[0, :, lanes(ps[j])] = y[j]
            state[ps[j]] = s0[j] * et[j] + jnp.where(same_head, upd[j], jnp.zeros_like(upd[j])) * em[j]
        return carry

    lax.fori_loop(0, n_slabs // unroll, body, 0)


def _rwkv_scan(r, v, kk, wl, b, kd):
    S, W = r.shape
    C = CHUNK
    nc = S // C
    n_slabs = W // LANE

    def shared(d, c):
        return (c + d * (nc - 1 - 2 * c), 0)

    def perdir(d, c):
        return (d, c + d * (nc - 1 - 2 * c), 0)

    s1 = pl.BlockSpec((C, W), shared)
    s2 = pl.BlockSpec((1, C, W), perdir)
    return pl.pallas_call(
        functools.partial(_scan_kernel, C=C, n_slabs=n_slabs, hs=HEAD_SIZE,
                          unroll=math.gcd(n_slabs, SCAN_UNROLL)),
        grid=(2, nc),
        in_specs=[s1, s1, s1, s2, s2, s2],
        out_specs=s2,
        out_shape=jax.ShapeDtypeStruct((2, S, W), F32),
        scratch_shapes=[pltpu.VMEM((n_slabs, LANE, LANE), F32)],
        compiler_params=_params(("arbitrary", "arbitrary")),
        name="rwkv_scan",
    )(r, v, kk, wl, b, kd)


def _fin_kernel(y_ref, g_ref, bg_ref, lg_ref, lb_ref, bd_ref, o_ref, *, hs):
    y = y_ref[0] + y_ref[1]
    bd = bd_ref[...]
    mean = _seg_sum(y, bd) * (1.0 / hs)
    yc = y - mean
    var = _seg_sum(yc * yc, bd) * (1.0 / hs)
    yn = yc * lax.rsqrt(var + GN_EPS) * lg_ref[...] + lb_ref[...]
    o_ref[...] = (yn * g_ref[...] + bg_ref[...]).astype(o_ref.dtype)


def _rwkv_finish(y2, g, bg, lw, T, cw):
    _, S, W = y2.shape
    one = pl.BlockSpec((T, cw), lambda i, j: (i, j))
    vec = pl.BlockSpec((1, cw), lambda i, j: (0, j))
    return pl.pallas_call(
        functools.partial(_fin_kernel, hs=HEAD_SIZE),
        grid=(S // T, W // cw),
        in_specs=[pl.BlockSpec((2, T, cw), lambda i, j: (0, i, j)), one, one, vec, vec,
                  pl.BlockSpec((3 * LANE, LANE), lambda i, j: (0, 0))],
        out_specs=one,
        out_shape=jax.ShapeDtypeStruct((S, W), BF16),
        compiler_params=_params(("parallel", "parallel")),
        name="rwkv_finish",
    )(y2, g, bg, lw["lnx_g"], lw["lnx_b"], lw["bd"])


def _dft1_kernel(x_ref, cs_ref, tc_ref, ts_ref, o_ref, *, n1):
    g = _dot(cs_ref[...], x_ref[...])
    gr, gi = g[:n1], g[n1:]
    reps = x_ref.shape[1] // LANE
    tc = jnp.concatenate([tc_ref[0]] * reps, axis=1)
    ts = jnp.concatenate([ts_ref[0]] * reps, axis=1)
    o_ref[0] = (gr * tc + gi * ts).astype(o_ref.dtype)
    o_ref[1] = (gi * tc - gr * ts).astype(o_ref.dtype)


def _dft2_kernel(g_ref, m2_ref, cc_ref, sc_ref, o_ref, *, n2, gw):
    fw = g_ref.shape[-1]
    g = g_ref[...].reshape(2 * n2, fw)
    ab = _dot(m2_ref[...], g)
    a = ab[:n2].astype(BF16)
    b = ab[n2:].astype(BF16)
    cc = cc_ref[...]
    sc = sc_ref[...]
    for q in range(fw // gw):
        sl = slice(q * gw, (q + 1) * gw)
        o_ref[:, sl] = (_dot(a[:, sl], cc) + _dot(b[:, sl], sc)).astype(o_ref.dtype)


def _fourier_tables(S, gw):
    n1 = 1 << (int(math.log2(S)) // 2)
    n2 = S // n1
    assert n1 * n2 == S
    i1 = np.arange(n1)
    ang1 = 2.0 * np.pi * np.outer(i1, i1) / n1
    cs1 = np.concatenate([np.cos(ang1), -np.sin(ang1)], axis=0)
    i2 = np.arange(n2)
    angt = 2.0 * np.pi * np.outer(i2, i1) / S
    ang2 = 2.0 * np.pi * np.outer(i2, i2) / n2
    c2, s2 = np.cos(ang2), np.sin(ang2)
    m2 = np.block([[c2, s2], [-s2, c2]])
    ic = np.arange(gw)
    angc = 2.0 * np.pi * np.outer(ic, ic) / gw
    scale = 1.0 / math.sqrt(S * gw)
    return dict(n1=n1, n2=n2,
                cs1=jnp.asarray(cs1, BF16),
                tc=jnp.asarray(np.cos(angt), F32), ts=jnp.asarray(np.sin(angt), F32),
                m2=jnp.asarray(m2, BF16),
                cc=jnp.asarray(np.cos(angc) * scale, BF16), sc=jnp.asarray(np.sin(angc) * scale, BF16))


def _fourier_mixer(pf, tab):
    S, Fw = pf.shape
    n1, n2 = tab["n1"], tab["n2"]
    gw = Fw // N_FOURIER_GROUPS
    tc = jnp.broadcast_to(tab["tc"][:, :, None], (n2, n1, LANE))
    ts = jnp.broadcast_to(tab["ts"][:, :, None], (n2, n1, LANE))
    x2 = pf.reshape(n1, n2 * Fw)
    g = pl.pallas_call(
        functools.partial(_dft1_kernel, n1=n1),
        grid=(n2,),
        in_specs=[pl.BlockSpec((n1, Fw), lambda j: (0, j)),
                  pl.BlockSpec((2 * n1, n1), lambda j: (0, 0)),
                  pl.BlockSpec((1, n1, LANE), lambda j: (j, 0, 0)),
                  pl.BlockSpec((1, n1, LANE), lambda j: (j, 0, 0))],
        out_specs=pl.BlockSpec((2, n1, Fw), lambda j: (0, 0, j)),
        out_shape=jax.ShapeDtypeStruct((2, n1, n2 * Fw), BF16),
        compiler_params=_params(("parallel",)),
        name="dft_stage1",
    )(x2, tab["cs1"], tc, ts)
    g4 = g.reshape(2, n1, n2, Fw)
    out = pl.pallas_call(
        functools.partial(_dft2_kernel, n2=n2, gw=gw),
        grid=(n1,),
        in_specs=[pl.BlockSpec((2, None, n2, Fw), lambda i: (0, i, 0, 0)),
                  pl.BlockSpec((2 * n2, 2 * n2), lambda i: (0, 0)),
                  pl.BlockSpec((gw, gw), lambda i: (0, 0)),
                  pl.BlockSpec((gw, gw), lambda i: (0, 0))],
        out_specs=pl.BlockSpec((n2, Fw), lambda i: (0, i)),
        out_shape=jax.ShapeDtypeStruct((n2, n1 * Fw), BF16),
        compiler_params=_params(("parallel",)),
        name="dft_stage2",
    )(g4, tab["m2"], tab["cc"], tab["sc"])
    return out.reshape(S, Fw)


def _acc_dot(o_ref, x, w_ref):
    D = o_ref.shape[1]
    pn = _pick(D, 512)
    for n in range(0, D, pn):
        o_ref[:, n:n + pn] += _dot(x, w_ref[:, n:n + pn])


def _residual_copy(h_hbm, h_buf, sem):
    T = h_buf.shape[0]
    rows = pl.ds(pl.multiple_of(pl.program_id(0) * T, T), T)
    return pltpu.make_async_copy(h_hbm.at[rows, :], h_buf, sem)


def _res_ln_finish(k, h_hbm, h_buf, sem, g_ref, b_ref, o_ref, ob_ref, alpha):
    @pl.when(k == pl.num_programs(1) - 1)
    def _():
        _residual_copy(h_hbm, h_buf, sem).wait()
        T = o_ref.shape[0]
        for r in range(0, T, 16):
            for r8 in (r, r + 8):
                rows = pl.ds(r8, 8)
                o_ref[rows, :] = _ln_rows(alpha * h_buf[rows, :] + o_ref[rows, :], g_ref[...], b_ref[...])
            ob_ref[pl.ds(r, 16), :] = o_ref[pl.ds(r, 16), :].astype(BF16)


def _proj_ln_kernel(xa_ref, xb_ref, wa_ref, wb_ref, h_hbm, g_ref, b_ref, o_ref, ob_ref, h_buf, sem,
                    *, alpha, tn):
    n = pl.program_id(1)

    @pl.when(n == 0)
    def _():
        _residual_copy(h_hbm, h_buf, sem).start()

    cols = pl.ds(pl.multiple_of(n * tn, LANE), tn)
    o_ref[:, cols] = _dot(xa_ref[...], wa_ref[...]) + _dot(xb_ref[...], wb_ref[...])
    _res_ln_finish(n, h_hbm, h_buf, sem, g_ref, b_ref, o_ref, ob_ref, alpha)


def _ffn_tail_kernel(ug, uv, lg, lv, fg, fv, cg, cv, bg, bv, w_ref, h_hbm, g_ref, b_ref, o_ref, ob_ref,
                     act_a, act_b, h_buf, sem, *, alpha):
    k = pl.program_id(1)

    @pl.when(k == 0)
    def _():
        _residual_copy(h_hbm, h_buf, sem).start()
        o_ref[...] = jnp.zeros_like(o_ref)
        act_a[...] = jnp.zeros_like(act_a)

    T, D = o_ref.shape
    tk = act_a.shape[1]
    pn = _pick(D, 512)
    n_pan = D // pn
    rb = _pick(T, 64, 16)
    i = pl.program_id(0)
    first_tile = i == 0
    last_tile = i == pl.num_programs(0) - 1
    row = lax.broadcasted_iota(jnp.int32, (rb, 1), 0)

    pk = 16

    def conv(u_ref, last_ref, first_ref, c_ref, cb_ref, r0, cols):
        u = u_ref[r0:r0 + rb, cols].astype(F32)
        if r0 == 0:
            up = jnp.where(first_tile, 0.0, last_ref[0, :, cols])
        else:
            up = u_ref[r0 - pk:r0, cols].astype(F32)[pk - 1:pk]
        if r0 + rb == T:
            un = jnp.where(last_tile, 0.0, first_ref[0, :, cols])
        else:
            un = u_ref[r0 + rb:r0 + rb + pk, cols].astype(F32)[0:1]
        prev = jnp.where(row == 0, up, pltpu.roll(u, 1, 0))
        nxt = jnp.where(row == rb - 1, un, pltpu.roll(u, rb - 1, 0))
        return (c_ref[0:1, cols] * prev + c_ref[1:2, cols] * u + c_ref[2:3, cols] * nxt
                + cb_ref[:, cols])

    blocks = [(r0, slice(c, c + LANE)) for c in range(0, tk, LANE) for r0 in range(0, T, rb)]

    def step(act_in, act_out):
        for q in range(n_pan):
            n = q * pn
            o_ref[:, n:n + pn] += _dot(act_in[...], w_ref[:, n:n + pn])
            for r0, cols in blocks[q * len(blocks) // n_pan:(q + 1) * len(blocks) // n_pan]:
                gate = conv(ug, lg, fg, cg, bg, r0, cols)
                val = conv(uv, lv, fv, cv, bv, r0, cols)
                gelu = 0.5 * gate * (1.0 + lax.erf(gate * (1.0 / math.sqrt(2.0))))
                act_out[r0:r0 + rb, cols] = (gelu * val).astype(BF16)

    @pl.when(k % 2 == 0)
    def _():
        step(act_a, act_b)

    @pl.when(k % 2 == 1)
    def _():
        step(act_b, act_a)

    _res_ln_finish(k, h_hbm, h_buf, sem, g_ref, b_ref, o_ref, ob_ref, alpha)


def _row_tile_specs(T, D):
    h_spec = pl.BlockSpec(memory_space=pl.ANY)
    vec = pl.BlockSpec((8, D), lambda i, k: (0, 0))
    out = pl.BlockSpec((T, D), lambda i, k: (i, 0))
    scratch = [pltpu.VMEM((T, D), F32), pltpu.SemaphoreType.DMA(())]
    return h_spec, vec, out, scratch


def _proj_ln(xa, xb, w, l, h, g, b, alpha, T=512, tn=512, name="proj_ln"):
    M, Ka = xa.shape
    Kb = xb.shape[1]
    D = w.shape[2]
    assert Ka == Kb and w.shape[1] == Ka + Kb
    T = _pick(M, T, 8)
    tn = _pick(D, tn)
    h_spec, vec, out, scratch = _row_tile_specs(T, D)

    def x_spec(K):
        return pl.BlockSpec((T, K), lambda i, n: (i, 0), pipeline_mode=pl.Buffered(1))

    return pl.pallas_call(
        functools.partial(_proj_ln_kernel, alpha=alpha, tn=tn),
        grid=(M // T, D // tn),
        in_specs=[x_spec(Ka), x_spec(Kb),
                  pl.BlockSpec((None, Ka, tn), lambda i, n: (l, 0, n)),
                  pl.BlockSpec((None, Kb, tn), lambda i, n: (l, 1, n)), h_spec, vec, vec],
        out_specs=(out, out),
        out_shape=(jax.ShapeDtypeStruct((M, D), F32), jax.ShapeDtypeStruct((M, D), BF16)),
        scratch_shapes=scratch,
        compiler_params=_params(("arbitrary", "arbitrary")),
        name=name,
    )(xa, xb, w, w, h, jnp.broadcast_to(g, (8, D)), jnp.broadcast_to(b, (8, D)))


def _ffn_tail(u, u_first, u_last, conv_w, conv_b, w_down, l, h, g, b, alpha, T, tk=512):
    S, N2 = u.shape
    dffp = N2 // 2
    D = w_down.shape[2]
    tk = _pick(dffp, tk)
    nk = dffp // tk
    nb = S // T
    h_spec, vec, out, scratch = _row_tile_specs(T, D)

    def made(k):
        return jnp.minimum(k, nk - 1)

    def main(off):
        return pl.BlockSpec((T, tk), lambda i, k: (i, made(k) + off * nk))

    def cvec(off, rows):
        return pl.BlockSpec((rows, tk), lambda i, k: (0, made(k) + off * nk))

    hg = _halo_specs(nb, tk, made)
    hv = _halo_specs(nb, tk, lambda k: made(k) + nk)
    return pl.pallas_call(
        functools.partial(_ffn_tail_kernel, alpha=alpha),
        grid=(nb, nk + 1),
        in_specs=[main(0), main(1), hg[0], hv[0], hg[1], hv[1],
                  cvec(0, 3), cvec(1, 3), cvec(0, 1), cvec(1, 1),
                  pl.BlockSpec((None, tk, D), lambda i, k: (l, jnp.maximum(k - 1, 0), 0)), h_spec, vec, vec],
        out_specs=(out, out),
        out_shape=(jax.ShapeDtypeStruct((S, D), F32), jax.ShapeDtypeStruct((S, D), BF16)),
        scratch_shapes=[pltpu.VMEM((T, tk), BF16), pltpu.VMEM((T, tk), BF16)] + scratch,
        compiler_params=_params(("arbitrary", "arbitrary")),
        name="ffn_tail",
    )(u, u, u_last, u_last, u_first, u_first, conv_w, conv_w, conv_b, conv_b, w_down, h,
      jnp.broadcast_to(g, (8, D)), jnp.broadcast_to(b, (8, D)))


def _stacked_weights(W, gl, dl, al, w_in, mu_shift):
    gp = _round_up(gl, LANE)
    lp = _round_up(max(dl, al), LANE)
    o = 3 * W

    def padc(a, n):
        return jnp.pad(a, [(0, 0)] * (a.ndim - 1) + [(0, n - a.shape[-1])])

    mu = mu_shift[:, None, :]
    segs = [(o, gl, gp), (o + gl, dl, lp), (o + gl + dl, dl, lp),
            (o + gl + 2 * dl, al, lp), (o + gl + 2 * dl + al, al, lp)]
    n_shift = o + gl + 2 * dl + 2 * al
    return dict(
        gp=gp, lp=lp,
        w_rkv=w_in[:, :, :o].astype(BF16),
        w_lora=jnp.concatenate([padc(w_in[:, :, s:s + n], p) for s, n, p in segs], axis=-1).astype(BF16),
        w_f=w_in[:, :, n_shift:].astype(BF16),
        mu_rkv=mu[:, :, :o],
        mu_lora=jnp.concatenate([padc(mu[:, :, s:s + n], p) for s, n, p in segs], axis=-1))


def _layer_weights(l, sw, W, decay_base, w_decay_up, aaa_base, w_aaa_up, g_lora_up, k_k, k_a, r_k,
                   lnx_g, lnx_b):
    gp, lp = sw["gp"], sw["lp"]

    def padr(a, n):
        return jnp.pad(a, ((0, 0), (0, n - a.shape[1]), (0, 0)))

    hs = HEAD_SIZE
    idx = np.arange(LANE)
    bd = jnp.asarray(np.tile((idx[:, None] // hs) == (idx[None, :] // hs), (3, 1)), BF16)
    return dict(
        gp=gp, lp=lp,
        decay_base=decay_base[l], aaa_base=aaa_base[l],
        w_decay_up=padr(w_decay_up[l], lp).astype(BF16), w_aaa_up=padr(w_aaa_up[l], lp).astype(BF16),
        g_up=jnp.pad(g_lora_up[l], ((0, gp - g_lora_up.shape[1]), (0, 0))).astype(BF16),
        k_k=k_k[l][None, :], k_a=k_a[l][None, :], r_k=r_k[l].reshape(1, W),
        lnx_g=lnx_g[l][None, :], lnx_b=lnx_b[l][None, :], bd=bd)


def kernel(x, ln0_g, ln0_b, w_in, mu_shift, decay_base, w_decay_up, aaa_base, w_aaa_up, g_lora_up, k_k, k_a, r_k, lnx_g, lnx_b, w_out, ln1_g, ln1_b, w_ffn_up, ffn_conv_w, ffn_conv_b, w_ffn_down, ln2_g, ln2_b):
    B, S, D = x.shape
    depth = w_in.shape[0]
    W = k_k.shape[1]
    gl, dl, al = g_lora_up.shape[1], w_decay_up.shape[2], w_aaa_up.shape[2]
    n_shift = 3 * W + gl + 2 * dl + 2 * al
    Fw = w_in.shape[2] - n_shift
    dff = w_ffn_down.shape[1]
    alpha = float((2 * depth) ** 0.25)
    assert B == 1 and S % CHUNK == 0 and W % LANE == 0 and LANE % HEAD_SIZE == 0

    tab = _fourier_tables(S, Fw // N_FOURIER_GROUPS)
    Tp = _pick(S, 256, 8)
    cwp = _pick(W, 512)
    Tf = _pick(S, 512, 8)
    dffp = _round_up(dff, 1024)
    pad = dffp - dff

    def pad_gv(a):
        cfg = [(0, 0)] * (a.ndim - 1) + [(0, pad)]
        return jnp.concatenate([jnp.pad(a[..., :dff], cfg), jnp.pad(a[..., dff:], cfg)], axis=-1)

    sw = _stacked_weights(W, gl, dl, al, w_in, mu_shift)
    w_out_b = w_out.astype(BF16)
    w_up_b = pad_gv(w_ffn_up).astype(BF16)
    w_down_b = jnp.pad(w_ffn_down, ((0, 0), (0, pad), (0, 0))).astype(BF16)
    conv_w, conv_b = pad_gv(ffn_conv_w), pad_gv(ffn_conv_b[:, None, :])

    h, hb = _layer_norm(x.reshape(S, D), ln0_g, ln0_b)
    for l in range(depth):
        lw = _layer_weights(l, sw, W, decay_base, w_decay_up, aaa_base, w_aaa_up, g_lora_up, k_k, k_a,
                            r_k, lnx_g, lnx_b)
        zrkv = _matmul_shift(hb, sw["w_rkv"], sw["mu_rkv"], l, name="in_rkv")
        zlora = _matmul_shift(hb, sw["w_lora"], sw["mu_lora"], l, name="in_lora")
        pf = _matmul(hb, sw["w_f"], l, BF16, name="in_fourier")
        r, v, kk, wl, b, kd, g, bg = _rwkv_prep(zrkv, zlora, lw, Tp, cwp)
        y2 = _rwkv_scan(r, v, kk, wl, b, kd)
        y_rwkv = _rwkv_finish(y2, g, bg, lw, Tf, cwp)
        y_four = _fourier_mixer(pf, tab)
        h, hb = _proj_ln(y_rwkv, y_four, w_out_b, l, h, ln1_g[l], ln1_b[l], alpha, T=Tf, name="out_proj_ln")
        u, u_first, u_last = _matmul(hb, w_up_b, l, BF16, name="ffn_up", edge_rows=Tf)
        h, hb = _ffn_tail(u, u_first, u_last, conv_w[l], conv_b[l], w_down_b, l, h, ln2_g[l], ln2_b[l],
                          alpha, Tf)
    return h.reshape(B, S, D)
```

```python
import functools
import math

import numpy as np
import jax
import jax.numpy as jnp
from jax import lax
from jax.experimental import pallas as pl
from jax.experimental.pallas import tpu as pltpu

F32 = jnp.float32
BF16 = jnp.bfloat16

LANE = 128
HEAD_SIZE = 64
N_FOURIER_GROUPS = 8
LN_EPS = 1e-5
GN_EPS = 64e-5
CHUNK = 64
SCAN_UNROLL = 16
VMEM_LIMIT = 56 * 1024 * 1024


def _round_up(n, m):
    return (n + m - 1) // m * m


def _pick(dim, target, align=LANE):
    if dim <= target:
        return dim
    best = None
    for t in range(align, target + 1, align):
        if dim % t == 0:
            best = t
    assert best is not None, (dim, target, align)
    return best


def _params(sem):
    return pltpu.CompilerParams(dimension_semantics=sem, vmem_limit_bytes=VMEM_LIMIT)


def _mm_kernel(x_ref, w_ref, o_ref):
    o_ref[...] = jnp.dot(x_ref[...], w_ref[...], preferred_element_type=F32).astype(o_ref.dtype)


def _mm_edges_kernel(x_ref, w_ref, o_ref, f_ref, l_ref, *, tb):
    res = jnp.dot(x_ref[...], w_ref[...], preferred_element_type=F32)
    o_ref[...] = res.astype(o_ref.dtype)
    for q in range(res.shape[0] // tb):
        f_ref[q] = res[q * tb:q * tb + 1, :]
        l_ref[q] = res[(q + 1) * tb - 1:(q + 1) * tb, :]


def _matmul(x, w, l, out_dtype, tm=1024, tn=1024, name="mm", edge_rows=None):
    M, K = x.shape
    N = w.shape[2]
    tm = _pick(M, tm, 8)
    tn = _pick(N, tn)
    in_specs = [pl.BlockSpec((tm, K), lambda j, i: (i, 0)),
                pl.BlockSpec((None, K, tn), lambda j, i: (l, 0, j))]
    out = pl.BlockSpec((tm, tn), lambda j, i: (i, j))
    if edge_rows is not None:
        assert tm % edge_rows == 0
        nq = tm // edge_rows
        edge = pl.BlockSpec((nq, 1, tn), lambda j, i: (i, 0, j))
        esd = jax.ShapeDtypeStruct((M // edge_rows, 1, N), F32)
        return pl.pallas_call(
            functools.partial(_mm_edges_kernel, tb=edge_rows),
            grid=(N // tn, M // tm),
            in_specs=in_specs,
            out_specs=(out, edge, edge),
            out_shape=(jax.ShapeDtypeStruct((M, N), out_dtype), esd, esd),
            compiler_params=_params(("parallel", "parallel")),
            name=name,
        )(x, w)
    return pl.pallas_call(
        _mm_kernel,
        grid=(N // tn, M // tm),
        in_specs=in_specs,
        out_specs=out,
        out_shape=jax.ShapeDtypeStruct((M, N), out_dtype),
        compiler_params=_params(("parallel", "parallel")),
        name=name,
    )(x, w)


BF16_ROWS = 16


def _mm_shift_kernel(x_ref, xp_ref, xn_ref, w_ref, mu_ref, o_ref, *, pn):
    i = pl.program_id(1)
    tm = x_ref.shape[0]
    xh = jnp.concatenate([xp_ref[...], xn_ref[...]], axis=0)
    def panel(n):
        cols = slice(n, n + pn)
        return _dot(x_ref[...], w_ref[:, cols]), _dot(xh, w_ref[:, cols])

    starts = list(range(0, o_ref.shape[1], pn))
    nxt_panel = panel(starts[0])
    for q, n in enumerate(starts):
        cols = slice(n, n + pn)
        p, ph = nxt_panel
        if q + 1 < len(starts):
            nxt_panel = panel(starts[q + 1])
        mu = mu_ref[:, cols]
        zp = jnp.where(i > 0, ph[BF16_ROWS - 1:BF16_ROWS], 0.0)
        zn = jnp.where(i < pl.num_programs(1) - 1, ph[BF16_ROWS:BF16_ROWS + 1], 0.0)
        o_ref[:, cols] = p + (0.5 * (pltpu.roll(p, 1, 0) + pltpu.roll(p, tm - 1, 0)) - p) * mu
        first, last = p[0:1], p[tm - 1:tm]
        o_ref[0:1, cols] = first + (0.5 * (zp + p[1:2]) - first) * mu
        o_ref[tm - 1:tm, cols] = last + (0.5 * (p[tm - 2:tm - 1] + zn) - last) * mu


def _matmul_shift(x, w, mu, l, n_cols=None, tm=1024, tn=1024, name="mm_shift"):
    M, K = x.shape
    N = w.shape[2] if n_cols is None else n_cols
    tm = _pick(M, tm, BF16_ROWS)
    tn = _pick(N, tn)
    nh = tm // BF16_ROWS
    last = M // BF16_ROWS - 1
    return pl.pallas_call(
        functools.partial(_mm_shift_kernel, pn=_pick(tn, 256)),
        grid=(N // tn, M // tm),
        in_specs=[pl.BlockSpec((tm, K), lambda j, i: (i, 0)),
                  pl.BlockSpec((BF16_ROWS, K), lambda j, i: (jnp.maximum(i * nh - 1, 0), 0)),
                  pl.BlockSpec((BF16_ROWS, K), lambda j, i: (jnp.minimum((i + 1) * nh, last), 0)),
                  pl.BlockSpec((None, K, tn), lambda j, i: (l, 0, j)),
                  pl.BlockSpec((None, 1, tn), lambda j, i: (l, 0, j))],
        out_specs=pl.BlockSpec((tm, tn), lambda j, i: (i, j)),
        out_shape=jax.ShapeDtypeStruct((M, N), F32),
        compiler_params=_params(("parallel", "parallel")),
        name=name,
    )(x, x, x, w, mu)


def _ln_rows(x, g, b):
    mu = jnp.mean(x, axis=-1, keepdims=True)
    xc = x - mu
    var = jnp.mean(xc * xc, axis=-1, keepdims=True)
    return xc * lax.rsqrt(var + LN_EPS) * g + b


def _ln_kernel(x_ref, g_ref, b_ref, o_ref, ob_ref):
    y = _ln_rows(x_ref[...], g_ref[...], b_ref[...])
    o_ref[...] = y
    ob_ref[...] = y.astype(BF16)


def _layer_norm(x, g, b):
    M, D = x.shape
    tm = _pick(M, 256, 8)
    row = pl.BlockSpec((tm, D), lambda i: (i, 0))
    vec = pl.BlockSpec((1, D), lambda i: (0, 0))
    return pl.pallas_call(
        _ln_kernel, grid=(M // tm,), in_specs=[row, vec, vec], out_specs=(row, row),
        out_shape=(jax.ShapeDtypeStruct((M, D), F32), jax.ShapeDtypeStruct((M, D), BF16)),
        compiler_params=_params(("parallel",)), name="ln",
    )(x, g.reshape(1, D), b.reshape(1, D))


def _halo_specs(T_blocks, cw, col):
    prev = pl.BlockSpec((1, 1, cw), lambda i, j: (jnp.maximum(i - 1, 0), 0, col(j)))
    nxt = pl.BlockSpec((1, 1, cw), lambda i, j: (jnp.minimum(i + 1, T_blocks - 1), 0, col(j)))
    return prev, nxt


def _split3(w):
    w1 = w.astype(BF16)
    res = w - w1.astype(F32)
    w2 = res.astype(BF16)
    w3 = (res - w2.astype(F32)).astype(BF16)
    return jnp.concatenate([w1, w2, w3], axis=1)


def _seg_sum(x, bd3):
    parts = [jnp.dot(_split3(x[:, i * LANE:(i + 1) * LANE]), bd3, preferred_element_type=F32)
             for i in range(x.shape[1] // LANE)]
    return parts[0] if len(parts) == 1 else jnp.concatenate(parts, axis=1)


def _sigmoid(x):
    return 1.0 / (1.0 + jnp.exp(-x))


def _prep_kernel(zr, zk, zv, zlo, dbase, wdu, abase, wau, gup, kkw, kaw, rkw, bd,
                 r_o, v_o, kk_o, wl_o, b_o, kd_o, g_o, bg_o, *, gp, lp):
    r = zr[...]
    k = zk[...]
    v = zv[...]
    zl = zlo[...]
    bdm = bd[...]

    g = jnp.dot(_sigmoid(zl[:, :gp]).astype(BF16), gup[...], preferred_element_type=F32)
    kk = k * kkw[...]
    kk = kk / jnp.maximum(jnp.sqrt(_seg_sum(kk * kk, bdm)), 1e-12)
    ka = kaw[...]
    ksum = jnp.zeros_like(k)
    for d in range(2):
        wd = jnp.tanh(zl[:, gp + d * lp:gp + (d + 1) * lp]).astype(BF16)
        x = dbase[d:d + 1, :] + jnp.dot(wd, wdu[d], preferred_element_type=F32)
        wl_o[d] = -math.exp(-0.5) * _sigmoid(x)
        ad = zl[:, gp + (2 + d) * lp:gp + (3 + d) * lp].astype(BF16)
        a = _sigmoid(abase[d:d + 1, :] + jnp.dot(ad, wau[d], preferred_element_type=F32))
        kd = k * (1.0 + (a - 1.0) * ka)
        kd_o[d] = kd.astype(kd_o.dtype)
        b_o[d] = (a * kk).astype(b_o.dtype)
        ksum = ksum + kd
    bonus = _seg_sum(r * ksum * rkw[...], bdm) * v
    r_o[...] = r.astype(r_o.dtype)
    v_o[...] = v.astype(v_o.dtype)
    kk_o[...] = kk.astype(kk_o.dtype)
    g_o[...] = g.astype(g_o.dtype)
    bg_o[...] = (bonus * g).astype(bg_o.dtype)


def _rwkv_prep(zrkv, zlora, lw, T, cw):
    S = zrkv.shape[0]
    W = zrkv.shape[1] // 3
    LW = zlora.shape[1]
    gp, lp = lw["gp"], lw["lp"]
    nb, ncb = S // T, W // cw

    def main(off):
        return pl.BlockSpec((T, cw), lambda i, j: (i, j + off * ncb))

    def vec(rows=1):
        return pl.BlockSpec((rows, cw), lambda i, j: (0, j))

    in_specs = [main(0), main(1), main(2),
                pl.BlockSpec((T, LW), lambda i, j: (i, 0)),
                vec(2),
                pl.BlockSpec((2, lp, cw), lambda i, j: (0, 0, j)),
                vec(2),
                pl.BlockSpec((2, lp, cw), lambda i, j: (0, 0, j)),
                pl.BlockSpec((gp, cw), lambda i, j: (0, j)),
                vec(), vec(), vec(),
                pl.BlockSpec((3 * LANE, LANE), lambda i, j: (0, 0))]
    one = pl.BlockSpec((T, cw), lambda i, j: (i, j))
    two = pl.BlockSpec((2, T, cw), lambda i, j: (0, i, j))
    sd1 = jax.ShapeDtypeStruct((S, W), BF16)
    sd2 = jax.ShapeDtypeStruct((2, S, W), BF16)
    return pl.pallas_call(
        functools.partial(_prep_kernel, gp=gp, lp=lp),
        grid=(nb, ncb),
        in_specs=in_specs,
        out_specs=(one, one, one, two, two, two, one, one),
        out_shape=(sd1, sd1, sd1, jax.ShapeDtypeStruct((2, S, W), F32), sd2, sd2, sd1, sd1),
        compiler_params=_params(("parallel", "parallel")),
        name="rwkv_prep",
    )(zrkv, zrkv, zrkv, zlora, lw["decay_base"], lw["w_decay_up"],
      lw["aaa_base"], lw["w_aaa_up"], lw["g_up"], lw["k_k"], lw["k_a"], lw["r_k"], lw["bd"])


def _dot_nt(a, b):
    return lax.dot_general(a, b, (((1,), (1,)), ((), ())), preferred_element_type=F32)


def _dot_tn(a, b):
    return lax.dot_general(a, b, (((0,), (0,)), ((), ())), preferred_element_type=F32)


def _dot(a, b):
    return jnp.dot(a, b, preferred_element_type=F32)


def _scan_kernel(r_ref, v_ref, kk_ref, wl_ref, b_ref, kd_ref, y_ref,
                 state, *, C, n_slabs, hs, unroll):
    d = pl.program_id(0)
    c = pl.program_id(1)

    @pl.when(c == 0)
    def _():
        state[...] = jnp.zeros_like(state)

    sgn = 1 - 2 * d
    ri = lax.broadcasted_iota(jnp.int32, (C, C), 0)
    ci = lax.broadcasted_iota(jnp.int32, (C, C), 1)
    tri = ((ri - ci) * sgn >= 0).astype(BF16)

    n_heads = LANE // hs
    R = n_heads * C
    lc = int(math.log2(C))
    lh = int(math.log2(hs))
    n_sq = lc
    ri = lax.broadcasted_iota(jnp.int32, (C, 2 * R), 0)
    ci = lax.broadcasted_iota(jnp.int32, (C, 2 * R), 1)
    diff = (ri - (ci & (C - 1))) * sgn
    strict = diff > 0
    incl = diff >= 0
    sr = lax.broadcasted_iota(jnp.int32, (LANE, LANE), 0)
    sc = lax.broadcasted_iota(jnp.int32, (LANE, LANE), 1)
    same_head = (sr >> lh) == (sc >> lh)

    def stack(x, shift):
        grp = lax.broadcasted_iota(jnp.int32, x.shape, 1) >> shift
        return jnp.concatenate(
            [jnp.where(grp == h, x, jnp.zeros_like(x)) for h in range(n_heads)], axis=0)

    def lanes(p):
        return pl.ds(pl.multiple_of(p * LANE, LANE), LANE)

    def body(q, carry):
        ps = [q * unroll + j for j in range(unroll)]
        n = range(unroll)
        wl = [wl_ref[0, :, lanes(p)] for p in ps]
        s0 = [state[p] for p in ps]
        v = [v_ref[:, lanes(p)].astype(BF16) for p in ps]
        cum3 = [_dot(tri, _split3(wl[j])) for j in n]
        cum = [cum3[j][:, :LANE] + cum3[j][:, LANE:2 * LANE] + cum3[j][:, 2 * LANE:] for j in n]
        tot = [jnp.sum(wl[j], axis=0, keepdims=True) for j in n]
        m = [0.5 * tot[j] for j in n]
        e_iv = [jnp.exp(m[j] - cum[j]) for j in n]
        at = [(-kk_ref[:, lanes(ps[j])] * jnp.exp(cum[j] - wl[j] - m[j])).astype(BF16) for j in n]
        rt = [(r_ref[:, lanes(ps[j])] * jnp.exp(cum[j] - m[j])).astype(BF16) for j in n]
        bt = [(b_ref[0, :, lanes(ps[j])] * e_iv[j]).astype(BF16) for j in n]
        kt = [(kd_ref[0, :, lanes(ps[j])] * e_iv[j]).astype(BF16) for j in n]
        em = [jnp.exp(m[j]) for j in n]
        et = [jnp.exp(tot[j]) for j in n]
        s0s = [(s0[j] * em[j]).astype(BF16) for j in n]
        ar = [jnp.concatenate([at[j], rt[j]], axis=0) for j in n]
        big = [_dot_nt(ar[j], jnp.concatenate([stack(bt[j], lh), stack(kt[j], lh)], axis=0))
               for j in n]
        sx = [_dot_nt(ar[j], s0s[j]) for j in n]
        top = [jnp.where(strict, big[j][:C], jnp.zeros((C, 2 * R), F32)) for j in n]
        lab = [top[j][:, :R].astype(BF16) for j in n]
        lak = [top[j][:, R:].astype(BF16) for j in n]
        arbk = [jnp.where(incl, big[j][C:], jnp.zeros((C, 2 * R), F32)).astype(BF16) for j in n]
        v_s = [stack(v[j], lh) for j in n]
        x = [sx[j][:C] + _dot(lak[j], v_s[j]) for j in n]
        for i in range(n_sq - 1):
            both = [_dot(lab[j], jnp.concatenate([stack(x[j].astype(BF16), lh), stack(lab[j], lc)], axis=1))
                    for j in n]
            x = [x[j] + both[j][:, :LANE] for j in n]
            lab = [both[j][:, LANE:].astype(BF16) for j in n]
        u = [(x[j] + _dot(lab[j], stack(x[j].astype(BF16), lh))).astype(BF16) for j in n]
        y = [sx[j][C:] + _dot(arbk[j], jnp.concatenate([stack(u[j], lh), v_s[j]], axis=0)) for j in n]
        upd = [_dot_tn(jnp.concatenate([u[j], v[j]], axis=0), jnp.concatenate([bt[j], kt[j]], axis=0))
               for j in n]
        for j in n:
            y_ref[0, :, lanes(ps[j])] = y[j]
            state[ps[j]] = s0[j] * et[j] + jnp.where(same_head, upd[j], jnp.zeros_like(upd[j])) * em[j]
        return carry

    lax.fori_loop(0, n_slabs // unroll, body, 0)


def _rwkv_scan(r, v, kk, wl, b, kd):
    S, W = r.shape
    C = CHUNK
    nc = S // C
    n_slabs = W // LANE

    def shared(d, c):
        return (c + d * (nc - 1 - 2 * c), 0)

    def perdir(d, c):
        return (d, c + d * (nc - 1 - 2 * c), 0)

    s1 = pl.BlockSpec((C, W), shared)
    s2 = pl.BlockSpec((1, C, W), perdir)
    return pl.pallas_call(
        functools.partial(_scan_kernel, C=C, n_slabs=n_slabs, hs=HEAD_SIZE,
                          unroll=math.gcd(n_slabs, SCAN_UNROLL)),
        grid=(2, nc),
        in_specs=[s1, s1, s1, s2, s2, s2],
        out_specs=s2,
        out_shape=jax.ShapeDtypeStruct((2, S, W), F32),
        scratch_shapes=[pltpu.VMEM((n_slabs, LANE, LANE), F32)],
        compiler_params=_params(("arbitrary", "arbitrary")),
        name="rwkv_scan",
    )(r, v, kk, wl, b, kd)


def _fin_kernel(y_ref, g_ref, bg_ref, lg_ref, lb_ref, bd_ref, o_ref, *, hs):
    y = y_ref[0] + y_ref[1]
    bd = bd_ref[...]
    mean = _seg_sum(y, bd) * (1.0 / hs)
    yc = y - mean
    var = _seg_sum(yc * yc, bd) * (1.0 / hs)
    yn = yc * lax.rsqrt(var + GN_EPS) * lg_ref[...] + lb_ref[...]
    o_ref[...] = (yn * g_ref[...] + bg_ref[...]).astype(o_ref.dtype)


def _rwkv_finish(y2, g, bg, lw, T, cw):
    _, S, W = y2.shape
    one = pl.BlockSpec((T, cw), lambda i, j: (i, j))
    vec = pl.BlockSpec((1, cw), lambda i, j: (0, j))
    return pl.pallas_call(
        functools.partial(_fin_kernel, hs=HEAD_SIZE),
        grid=(S // T, W // cw),
        in_specs=[pl.BlockSpec((2, T, cw), lambda i, j: (0, i, j)), one, one, vec, vec,
                  pl.BlockSpec((3 * LANE, LANE), lambda i, j: (0, 0))],
        out_specs=one,
        out_shape=jax.ShapeDtypeStruct((S, W), BF16),
        compiler_params=_params(("parallel", "parallel")),
        name="rwkv_finish",
    )(y2, g, bg, lw["lnx_g"], lw["lnx_b"], lw["bd"])


def _dft1_kernel(x_ref, cs_ref, tc_ref, ts_ref, o_ref, *, n1):
    g = _dot(cs_ref[...], x_ref[...])
    gr, gi = g[:n1], g[n1:]
    reps = x_ref.shape[1] // LANE
    tc = jnp.concatenate([tc_ref[0]] * reps, axis=1)
    ts = jnp.concatenate([ts_ref[0]] * reps, axis=1)
    o_ref[0] = (gr * tc + gi * ts).astype(o_ref.dtype)
    o_ref[1] = (gi * tc - gr * ts).astype(o_ref.dtype)


def _dft2_kernel(g_ref, m2_ref, cc_ref, sc_ref, o_ref, *, n2, gw):
    fw = g_ref.shape[-1]
    g = g_ref[...].reshape(2 * n2, fw)
    ab = _dot(m2_ref[...], g)
    a = ab[:n2].astype(BF16)
    b = ab[n2:].astype(BF16)
    cc = cc_ref[...]
    sc = sc_ref[...]
    for q in range(fw // gw):
        sl = slice(q * gw, (q + 1) * gw)
        o_ref[:, sl] = (_dot(a[:, sl], cc) + _dot(b[:, sl], sc)).astype(o_ref.dtype)


def _fourier_tables(S, gw):
    n1 = 1 << (int(math.log2(S)) // 2)
    n2 = S // n1
    assert n1 * n2 == S
    i1 = np.arange(n1)
    ang1 = 2.0 * np.pi * np.outer(i1, i1) / n1
    cs1 = np.concatenate([np.cos(ang1), -np.sin(ang1)], axis=0)
    i2 = np.arange(n2)
    angt = 2.0 * np.pi * np.outer(i2, i1) / S
    ang2 = 2.0 * np.pi * np.outer(i2, i2) / n2
    c2, s2 = np.cos(ang2), np.sin(ang2)
    m2 = np.block([[c2, s2], [-s2, c2]])
    ic = np.arange(gw)
    angc = 2.0 * np.pi * np.outer(ic, ic) / gw
    scale = 1.0 / math.sqrt(S * gw)
    return dict(n1=n1, n2=n2,
                cs1=jnp.asarray(cs1, BF16),
                tc=jnp.asarray(np.cos(angt), F32), ts=jnp.asarray(np.sin(angt), F32),
                m2=jnp.asarray(m2, BF16),
                cc=jnp.asarray(np.cos(angc) * scale, BF16), sc=jnp.asarray(np.sin(angc) * scale, BF16))


def _fourier_mixer(pf, tab):
    S, Fw = pf.shape
    n1, n2 = tab["n1"], tab["n2"]
    gw = Fw // N_FOURIER_GROUPS
    tc = jnp.broadcast_to(tab["tc"][:, :, None], (n2, n1, LANE))
    ts = jnp.broadcast_to(tab["ts"][:, :, None], (n2, n1, LANE))
    x2 = pf.reshape(n1, n2 * Fw)
    g = pl.pallas_call(
        functools.partial(_dft1_kernel, n1=n1),
        grid=(n2,),
        in_specs=[pl.BlockSpec((n1, Fw), lambda j: (0, j)),
                  pl.BlockSpec((2 * n1, n1), lambda j: (0, 0)),
                  pl.BlockSpec((1, n1, LANE), lambda j: (j, 0, 0)),
                  pl.BlockSpec((1, n1, LANE), lambda j: (j, 0, 0))],
        out_specs=pl.BlockSpec((2, n1, Fw), lambda j: (0, 0, j)),
        out_shape=jax.ShapeDtypeStruct((2, n1, n2 * Fw), BF16),
        compiler_params=_params(("parallel",)),
        name="dft_stage1",
    )(x2, tab["cs1"], tc, ts)
    g4 = g.reshape(2, n1, n2, Fw)
    out = pl.pallas_call(
        functools.partial(_dft2_kernel, n2=n2, gw=gw),
        grid=(n1,),
        in_specs=[pl.BlockSpec((2, None, n2, Fw), lambda i: (0, i, 0, 0)),
                  pl.BlockSpec((2 * n2, 2 * n2), lambda i: (0, 0)),
                  pl.BlockSpec((gw, gw), lambda i: (0, 0)),
                  pl.BlockSpec((gw, gw), lambda i: (0, 0))],
        out_specs=pl.BlockSpec((n2, Fw), lambda i: (0, i)),
        out_shape=jax.ShapeDtypeStruct((n2, n1 * Fw), BF16),
        compiler_params=_params(("parallel",)),
        name="dft_stage2",
    )(g4, tab["m2"], tab["cc"], tab["sc"])
    return out.reshape(S, Fw)


def _acc_dot(o_ref, x, w_ref):
    D = o_ref.shape[1]
    pn = _pick(D, 512)
    for n in range(0, D, pn):
        o_ref[:, n:n + pn] += _dot(x, w_ref[:, n:n + pn])


def _residual_copy(h_hbm, h_buf, sem):
    T = h_buf.shape[0]
    rows = pl.ds(pl.multiple_of(pl.program_id(0) * T, T), T)
    return pltpu.make_async_copy(h_hbm.at[rows, :], h_buf, sem)


def _res_ln_finish(k, h_hbm, h_buf, sem, g_ref, b_ref, o_ref, ob_ref, alpha):
    @pl.when(k == pl.num_programs(1) - 1)
    def _():
        _residual_copy(h_hbm, h_buf, sem).wait()
        T = o_ref.shape[0]
        for r in range(0, T, 16):
            for r8 in (r, r + 8):
                rows = pl.ds(r8, 8)
                o_ref[rows, :] = _ln_rows(alpha * h_buf[rows, :] + o_ref[rows, :], g_ref[...], b_ref[...])
            ob_ref[pl.ds(r, 16), :] = o_ref[pl.ds(r, 16), :].astype(BF16)


def _proj_ln_kernel(xa_ref, xb_ref, wa_ref, wb_ref, h_hbm, g_ref, b_ref, o_ref, ob_ref, h_buf, sem,
                    *, alpha, tn):
    n = pl.program_id(1)

    @pl.when(n == 0)
    def _():
        _residual_copy(h_hbm, h_buf, sem).start()

    cols = pl.ds(pl.multiple_of(n * tn, LANE), tn)
    o_ref[:, cols] = _dot(xa_ref[...], wa_ref[...]) + _dot(xb_ref[...], wb_ref[...])
    _res_ln_finish(n, h_hbm, h_buf, sem, g_ref, b_ref, o_ref, ob_ref, alpha)


def _ffn_tail_kernel(ug, uv, lg, lv, fg, fv, cg, cv, bg, bv, w_ref, h_hbm, g_ref, b_ref, o_ref, ob_ref,
                     act_a, act_b, h_buf, sem, *, alpha):
    k = pl.program_id(1)

    @pl.when(k == 0)
    def _():
        _residual_copy(h_hbm, h_buf, sem).start()
        o_ref[...] = jnp.zeros_like(o_ref)
        act_a[...] = jnp.zeros_like(act_a)

    T, D = o_ref.shape
    tk = act_a.shape[1]
    pn = _pick(D, 512)
    n_pan = D // pn
    rb = _pick(T, 64, 16)
    i = pl.program_id(0)
    first_tile = i == 0
    last_tile = i == pl.num_programs(0) - 1
    row = lax.broadcasted_iota(jnp.int32, (rb, 1), 0)

    pk = 16

    def conv(u_ref, last_ref, first_ref, c_ref, cb_ref, r0, cols):
        u = u_ref[r0:r0 + rb, cols].astype(F32)
        if r0 == 0:
            up = jnp.where(first_tile, 0.0, last_ref[0, :, cols])
        else:
            up = u_ref[r0 - pk:r0, cols].astype(F32)[pk - 1:pk]
        if r0 + rb == T:
            un = jnp.where(last_tile, 0.0, first_ref[0, :, cols])
        else:
            un = u_ref[r0 + rb:r0 + rb + pk, cols].astype(F32)[0:1]
        prev = jnp.where(row == 0, up, pltpu.roll(u, 1, 0))
        nxt = jnp.where(row == rb - 1, un, pltpu.roll(u, rb - 1, 0))
        return (c_ref[0:1, cols] * prev + c_ref[1:2, cols] * u + c_ref[2:3, cols] * nxt
                + cb_ref[:, cols])

    blocks = [(r0, slice(c, c + LANE)) for c in range(0, tk, LANE) for r0 in range(0, T, rb)]

    def step(act_in, act_out):
        for q in range(n_pan):
            n = q * pn
            o_ref[:, n:n + pn] += _dot(act_in[...], w_ref[:, n:n + pn])
            for r0, cols in blocks[q * len(blocks) // n_pan:(q + 1) * len(blocks) // n_pan]:
                gate = conv(ug, lg, fg, cg, bg, r0, cols)
                val = conv(uv, lv, fv, cv, bv, r0, cols)
                gelu = 0.5 * gate * (1.0 + lax.erf(gate * (1.0 / math.sqrt(2.0))))
                act_out[r0:r0 + rb, cols] = (gelu * val).astype(BF16)

    @pl.when(k % 2 == 0)
    def _():
        step(act_a, act_b)

    @pl.when(k % 2 == 1)
    def _():
        step(act_b, act_a)

    _res_ln_finish(k, h_hbm, h_buf, sem, g_ref, b_ref, o_ref, ob_ref, alpha)


def _row_tile_specs(T, D):
    h_spec = pl.BlockSpec(memory_space=pl.ANY)
    vec = pl.BlockSpec((8, D), lambda i, k: (0, 0))
    out = pl.BlockSpec((T, D), lambda i, k: (i, 0))
    scratch = [pltpu.VMEM((T, D), F32), pltpu.SemaphoreType.DMA(())]
    return h_spec, vec, out, scratch


def _proj_ln(xa, xb, w, l, h, g, b, alpha, T=512, tn=512, name="proj_ln"):
    M, Ka = xa.shape
    Kb = xb.shape[1]
    D = w.shape[2]
    assert Ka == Kb and w.shape[1] == Ka + Kb
    T = _pick(M, T, 8)
    tn = _pick(D, tn)
    h_spec, vec, out, scratch = _row_tile_specs(T, D)

    def x_spec(K):
        return pl.BlockSpec((T, K), lambda i, n: (i, 0), pipeline_mode=pl.Buffered(1))

    return pl.pallas_call(
        functools.partial(_proj_ln_kernel, alpha=alpha, tn=tn),
        grid=(M // T, D // tn),
        in_specs=[x_spec(Ka), x_spec(Kb),
                  pl.BlockSpec((None, Ka, tn), lambda i, n: (l, 0, n)),
                  pl.BlockSpec((None, Kb, tn), lambda i, n: (l, 1, n)), h_spec, vec, vec],
        out_specs=(out, out),
        out_shape=(jax.ShapeDtypeStruct((M, D), F32), jax.ShapeDtypeStruct((M, D), BF16)),
        scratch_shapes=scratch,
        compiler_params=_params(("arbitrary", "arbitrary")),
        name=name,
    )(xa, xb, w, w, h, jnp.broadcast_to(g, (8, D)), jnp.broadcast_to(b, (8, D)))


def _ffn_tail(u, u_first, u_last, conv_w, conv_b, w_down, l, h, g, b, alpha, T, tk=512):
    S, N2 = u.shape
    dffp = N2 // 2
    D = w_down.shape[2]
    tk = _pick(dffp, tk)
    nk = dffp // tk
    nb = S // T
    h_spec, vec, out, scratch = _row_tile_specs(T, D)

    def made(k):
        return jnp.minimum(k, nk - 1)

    def main(off):
        return pl.BlockSpec((T, tk), lambda i, k: (i, made(k) + off * nk))

    def cvec(off, rows):
        return pl.BlockSpec((rows, tk), lambda i, k: (0, made(k) + off * nk))

    hg = _halo_specs(nb, tk, made)
    hv = _halo_specs(nb, tk, lambda k: made(k) + nk)
    return pl.pallas_call(
        functools.partial(_ffn_tail_kernel, alpha=alpha),
        grid=(nb, nk + 1),
        in_specs=[main(0), main(1), hg[0], hv[0], hg[1], hv[1],
                  cvec(0, 3), cvec(1, 3), cvec(0, 1), cvec(1, 1),
                  pl.BlockSpec((None, tk, D), lambda i, k: (l, jnp.maximum(k - 1, 0), 0)), h_spec, vec, vec],
        out_specs=(out, out),
        out_shape=(jax.ShapeDtypeStruct((S, D), F32), jax.ShapeDtypeStruct((S, D), BF16)),
        scratch_shapes=[pltpu.VMEM((T, tk), BF16), pltpu.VMEM((T, tk), BF16)] + scratch,
        compiler_params=_params(("arbitrary", "arbitrary")),
        name="ffn_tail",
    )(u, u, u_last, u_last, u_first, u_first, conv_w, conv_w, conv_b, conv_b, w_down, h,
      jnp.broadcast_to(g, (8, D)), jnp.broadcast_to(b, (8, D)))


def _stacked_weights(W, gl, dl, al, w_in, mu_shift):
    gp = _round_up(gl, LANE)
    lp = _round_up(max(dl, al), LANE)
    o = 3 * W

    def padc(a, n):
        return jnp.pad(a, [(0, 0)] * (a.ndim - 1) + [(0, n - a.shape[-1])])

    mu = mu_shift[:, None, :]
    w_b = w_in.astype(BF16)
    segs =[(o, gl, gp), (o + gl, dl, lp), (o + gl + dl, dl, lp),
            (o + gl + 2 * dl, al, lp), (o + gl + 2 * dl + al, al, lp)]
    n_shift = o + gl + 2 * dl + 2 * al
    return dict(
        gp=gp, lp=lp,
        w_rkv=w_b,
        w_lora=jnp.concatenate([padc(w_b[:, :, s:s + n], p) for s, n, p in segs], axis=-1),
        w_f=w_b[:, :, n_shift:],
        mu_rkv=mu,
        mu_lora=jnp.concatenate([padc(mu[:, :, s:s + n], p) for s, n, p in segs], axis=-1))


def _layer_weights(l, sw, W, decay_base, w_decay_up, aaa_base, w_aaa_up, g_lora_up, k_k, k_a, r_k,
                   lnx_g, lnx_b):
    gp, lp = sw["gp"], sw["lp"]

    def padr(a, n):
        return jnp.pad(a, ((0, 0), (0, n - a.shape[1]), (0, 0)))

    hs = HEAD_SIZE
    idx = np.arange(LANE)
    bd = jnp.asarray(np.tile((idx[:, None] // hs) == (idx[None, :] // hs), (3, 1)), BF16)
    return dict(
        gp=gp, lp=lp,
        decay_base=decay_base[l], aaa_base=aaa_base[l],
        w_decay_up=padr(w_decay_up[l], lp).astype(BF16), w_aaa_up=padr(w_aaa_up[l], lp).astype(BF16),
        g_up=jnp.pad(g_lora_up[l], ((0, gp - g_lora_up.shape[1]), (0, 0))).astype(BF16),
        k_k=k_k[l][None, :], k_a=k_a[l][None, :], r_k=r_k[l].reshape(1, W),
        lnx_g=lnx_g[l][None, :], lnx_b=lnx_b[l][None, :], bd=bd)


def kernel(x, ln0_g, ln0_b, w_in, mu_shift, decay_base, w_decay_up, aaa_base, w_aaa_up, g_lora_up, k_k, k_a, r_k, lnx_g, lnx_b, w_out, ln1_g, ln1_b, w_ffn_up, ffn_conv_w, ffn_conv_b, w_ffn_down, ln2_g, ln2_b):
    B, S, D = x.shape
    depth = w_in.shape[0]
    W = k_k.shape[1]
    gl, dl, al = g_lora_up.shape[1], w_decay_up.shape[2], w_aaa_up.shape[2]
    n_shift = 3 * W + gl + 2 * dl + 2 * al
    Fw = w_in.shape[2] - n_shift
    dff = w_ffn_down.shape[1]
    alpha = float((2 * depth) ** 0.25)
    assert B == 1 and S % CHUNK == 0 and W % LANE == 0 and LANE % HEAD_SIZE == 0

    tab = _fourier_tables(S, Fw // N_FOURIER_GROUPS)
    Tp = _pick(S, 256, 8)
    cwp = _pick(W, 512)
    Tf = _pick(S, 512, 8)
    dffp = _round_up(dff, 1024)
    pad = dffp - dff

    def pad_gv(a):
        cfg = [(0, 0)] * (a.ndim - 1) + [(0, pad)]
        return jnp.concatenate([jnp.pad(a[..., :dff], cfg), jnp.pad(a[..., dff:], cfg)], axis=-1)

    sw = _stacked_weights(W, gl, dl, al, w_in, mu_shift)
    w_out_b = w_out.astype(BF16)
    w_up_b = pad_gv(w_ffn_up).astype(BF16)
    w_down_b = jnp.pad(w_ffn_down, ((0, 0), (0, pad), (0, 0))).astype(BF16)
    conv_w, conv_b = pad_gv(ffn_conv_w), pad_gv(ffn_conv_b[:, None, :])

    h, hb = _layer_norm(x.reshape(S, D), ln0_g, ln0_b)
    for l in range(depth):
        lw = _layer_weights(l, sw, W, decay_base, w_decay_up, aaa_base, w_aaa_up, g_lora_up, k_k, k_a,
                            r_k, lnx_g, lnx_b)
        zrkv = _matmul_shift(hb, sw["w_rkv"], sw["mu_rkv"], l, n_cols=3 * W, name="in_rkv")
        zlora = _matmul_shift(hb, sw["w_lora"], sw["mu_lora"], l, name="in_lora")
        pf = _matmul(hb, sw["w_f"], l, BF16, name="in_fourier")
        r, v, kk, wl, b, kd, g, bg = _rwkv_prep(zrkv, zlora, lw, Tp, cwp)
        y2 = _rwkv_scan(r, v, kk, wl, b, kd)
        y_rwkv = _rwkv_finish(y2, g, bg, lw, Tf, cwp)
        y_four = _fourier_mixer(pf, tab)
        h, hb = _proj_ln(y_rwkv, y_four, w_out_b, l, h, ln1_g[l], ln1_b[l], alpha, T=Tf, name="out_proj_ln")
        u, u_first, u_last = _matmul(hb, w_up_b, l, BF16, name="ffn_up", edge_rows=Tf)
        h, hb = _ffn_tail(u, u_first, u_last, conv_w[l], conv_b[l], w_down_b, l, h, ln2_g[l], ln2_b[l],
                          alpha, Tf)
    return h.reshape(B, S, D)
```
